```python
import math
import jax, jax.numpy as jnp
from jax import lax
import numpy as np

D_MODEL = 4096
BATCH = 32
SEQ = 256
DEPTH = 2
DEC_BATCH = 4
DEC_SEQ = 2048
PAST_LEN = 256

GRID_W = 64
ROPE_BASE = 10000.0
HEAD_DIM = 128
QBLK = 128
D_RNN = D_MODEL
N_RNN_BLOCKS = 16
RNN_BLOCK = D_RNN // N_RNN_BLOCKS
CONV_W = 4
CONV_LEFT = 2
RG_C = 8.0
N_WIN_HEADS = D_MODEL // HEAD_DIM
N_WIN_KV = N_WIN_HEADS // 4
WINDOW = 128
WBLK = 128
WIN_WIDTH = N_WIN_HEADS * HEAD_DIM
N_DIFF_HEADS = D_MODEL // (2 * HEAD_DIM)
DIFF_WIDTH = N_DIFF_HEADS * 2 * HEAD_DIM
N_BRANCHES = 3
IN_SECTIONS = (D_RNN, D_RNN, WIN_WIDTH, N_WIN_KV * HEAD_DIM, N_WIN_KV * HEAD_DIM,
               DIFF_WIDTH, DIFF_WIDTH, DIFF_WIDTH, N_BRANCHES * D_MODEL)
D_IN = D_RNN * 2 + WIN_WIDTH + 2 * N_WIN_KV * HEAD_DIM + 3 * DIFF_WIDTH + N_BRANCHES * D_MODEL
MIX_WIDTH = D_RNN + WIN_WIDTH + DIFF_WIDTH
N_GROUPS = 4
EXPERTS_PER_GROUP = 8
N_EXPERTS = N_GROUPS * EXPERTS_PER_GROUP
TOP_K = 2
D_EXPERT = D_MODEL // 4
MOE_BLOCK = 128

EPS = 1e-6
NEG_INF = -1e30

kernel_name = "hybrid_diffusion_prefix_trunk_step"

F32 = jnp.float32


def rmsnorm(x, g):
    x32 = x.astype(F32)
    y = x32 * lax.rsqrt(jnp.mean(x32 * x32, axis=-1, keepdims=True) + EPS) * g.astype(F32)
    return y.astype(x.dtype)


def ada_mod(cvec, w, b):
    m = jax.nn.silu(cvec) @ w + b
    return jnp.split(m[:, None, :], 6, axis=-1)


def axial_rope(T):
    rows = T // GRID_W
    row = jnp.repeat(jnp.arange(rows), GRID_W).astype(F32)
    col = jnp.tile(jnp.arange(GRID_W), rows).astype(F32)
    n_freq = HEAD_DIM // 4
    inv = 1.0 / (ROPE_BASE ** (jnp.arange(n_freq, dtype=F32) / n_freq))
    ang = jnp.concatenate([row[:, None] * inv, col[:, None] * inv], axis=-1)
    return jnp.cos(ang), jnp.sin(ang)


def apply_rope(x, cos, sin):
    shape = (1, x.shape[1]) + (1,) * (x.ndim - 3) + (cos.shape[-1],)
    cos, sin = cos.reshape(shape), sin.reshape(shape)
    x1, x2 = jnp.split(x.astype(F32), 2, axis=-1)
    return jnp.concatenate([x1 * cos - x2 * sin, x2 * cos + x1 * sin], axis=-1).astype(x.dtype)


def to_blocks(q):
    B, T = q.shape[:2]
    return jnp.moveaxis(q.reshape((B, T // QBLK, QBLK) + q.shape[2:]), 1, 0)


def from_blocks(o):
    o = jnp.moveaxis(o, 0, 1)
    return o.reshape((o.shape[0], o.shape[1] * o.shape[2]) + o.shape[3:])


def centred_conv(x, w, b):
    T = x.shape[1]
    xp = jnp.pad(x, ((0, 0), (CONV_LEFT, CONV_W - 1 - CONV_LEFT), (0, 0)))
    y = b
    for j in range(CONV_W):
        y = y + xp[:, j:j + T] * w[j]
    return y


def _lin_combine(left, right):
    a_l, b_l = left
    a_r, b_r = right
    return a_l * a_r, a_r * b_l + b_r


def rglru_scan(x, w_a, b_a, w_x, b_x, lam, h0, reverse):
    B, T, _ = x.shape
    xb = x.reshape(B, T, N_RNN_BLOCKS, RNN_BLOCK)
    r = jax.nn.sigmoid((jnp.einsum('btnc,ncd->btnd', xb, w_a).reshape(B, T, D_RNN) + b_a).astype(F32))
    i = jax.nn.sigmoid((jnp.einsum('btnc,ncd->btnd', xb, w_x).reshape(B, T, D_RNN) + b_x).astype(F32))
    log_a = -RG_C * r * jax.nn.softplus(-lam.astype(F32))
    a = jnp.exp(log_a)
    u = jnp.sqrt(-jnp.expm1(2.0 * log_a)) * i * x.astype(F32)
    h0 = h0.astype(F32)
    if reverse:
        u = u.at[:, -1].add(a[:, -1] * h0)
    else:
        u = u.at[:, 0].add(a[:, 0] * h0)
    _, h = lax.associative_scan(_lin_combine, (a, u), reverse=reverse, axis=1)
    return h


def sink_attention(q, k, v, sink):
    B, T, HQ, dh = q.shape
    HKV = k.shape[2]
    G = HQ // HKV
    scale = dh ** -0.5
    sink_l = sink.astype(F32).reshape(HKV, G)

    def one(qb):
        qb = qb.reshape(B, QBLK, HKV, G, dh)
        s = jnp.einsum('bqhgd,bkhd->bhgqk', qb, k).astype(F32) * scale
        s_sink = jnp.broadcast_to(sink_l[None, :, :, None, None], s.shape[:-1] + (1,))
        p = jax.nn.softmax(jnp.concatenate([s, s_sink], axis=-1), axis=-1)[..., :-1]
        o = jnp.einsum('bhgqk,bkhd->bqhgd', p.astype(v.dtype), v)
        return o.reshape(B, QBLK, HQ, dh)

    return from_blocks(lax.map(one, to_blocks(q)))


def window_sink_attention(q, k, v, ck, cv, sink):
    B, T, HQ, dh = q.shape
    HKV = k.shape[2]
    G = HQ // HKV
    C = ck.shape[1]
    nb = T // WBLK
    scale = dh ** -0.5
    qb = jnp.moveaxis(q.reshape(B, nb, WBLK, HKV, G, dh), 1, 0)

    def band(z):
        zp = jnp.pad(z, ((0, 0), (WBLK, WBLK), (0, 0), (0, 0))).reshape(B, nb + 2, WBLK, HKV, dh)
        zb = jnp.concatenate([zp[:, :-2], zp[:, 1:-1], zp[:, 2:]], axis=2)
        return jnp.moveaxis(zb, 1, 0)

    kb, vb = band(k), band(v)
    rel = jnp.arange(3 * WBLK)[None, :] - WBLK - jnp.arange(WBLK)[:, None]
    kpos = (jnp.arange(nb)[:, None] - 1) * WBLK + jnp.arange(3 * WBLK)[None, :]
    valid = (jnp.abs(rel)[None] <= WINDOW) & ((kpos >= 0) & (kpos < T))[:, None, :]
    sink_l = sink.astype(F32).reshape(HKV, G)

    def one(args):
        qi, ki, vi, mi = args
        s_ctx = jnp.einsum('bqhgd,bchd->bhgqc', qi, ck).astype(F32) * scale
        s_loc = jnp.einsum('bqhgd,bkhd->bhgqk', qi, ki).astype(F32) * scale
        s_loc = jnp.where(mi, s_loc, NEG_INF)
        s_sink = jnp.broadcast_to(sink_l[None, :, :, None, None], s_ctx.shape[:-1] + (1,))
        p = jax.nn.softmax(jnp.concatenate([s_ctx, s_loc, s_sink], axis=-1), axis=-1)
        o = (jnp.einsum('bhgqc,bchd->bqhgd', p[..., :C].astype(vi.dtype), cv)
             + jnp.einsum('bhgqk,bkhd->bqhgd', p[..., C:C + 3 * WBLK].astype(vi.dtype), vi))
        return o.reshape(B, WBLK, HQ, dh)

    o = lax.map(one, (qb, kb, vb, valid))
    return jnp.moveaxis(o, 0, 1).reshape(B, T, HQ, dh)


def diff_attention(q, k, v, lam, lam_init, subln_g):
    dh = q.shape[-1]
    scale = dh ** -0.5

    def one(qi):
        s = jnp.einsum('bqhcd,bkhcd->bhcqk', qi, k).astype(F32) * scale
        p = jax.nn.softmax(s, axis=-1)
        a = p[:, :, 0] - lam * p[:, :, 1]
        return jnp.einsum('bhqk,bkhe->bqhe', a.astype(v.dtype), v)

    o = from_blocks(lax.map(one, to_blocks(q))).astype(F32)
    o = o * lax.rsqrt(jnp.mean(o * o, axis=-1, keepdims=True) + EPS) * subln_g.astype(F32) * (1.0 - lam_init)
    return o.astype(q.dtype)


def token_mixers(h, lp, lam_init, cache):
    B, T, _ = h.shape
    splits = [int(s) for s in np.cumsum(IN_SECTIONS)[:-1]]
    rx, rgate, wq, wk, wv, dq, dk, dv, mg = jnp.split(h @ lp['w_in'], splits, axis=-1)
    latent = cache is not None
    xc = centred_conv(rx, lp['conv_w'], lp['conv_b'])
    h0 = cache[4].astype(F32) if latent else jnp.zeros((B, 2, D_RNN), F32)
    h_fwd = rglru_scan(xc, lp['rg_wa'][0], lp['rg_ba'][0], lp['rg_wx'][0], lp['rg_bx'][0],
                       lp['rg_lambda'][0], h0[:, 0], reverse=False)
    h_bwd = rglru_scan(xc, lp['rg_wa'][1], lp['rg_ba'][1], lp['rg_wx'][1], lp['rg_bx'][1],
                       lp['rg_lambda'][1], h0[:, 1], reverse=True)
    o_a = (h_fwd + h_bwd).astype(h.dtype) * jax.nn.gelu(rgate)
    wq = wq.reshape(B, T, N_WIN_HEADS, HEAD_DIM)
    wk = wk.reshape(B, T, N_WIN_KV, HEAD_DIM)
    wv = wv.reshape(B, T, N_WIN_KV, HEAD_DIM)
    dq = dq.reshape(B, T, N_DIFF_HEADS, 2, HEAD_DIM)
    dk = dk.reshape(B, T, N_DIFF_HEADS, 2, HEAD_DIM)
    dv = dv.reshape(B, T, N_DIFF_HEADS, 2 * HEAD_DIM)
    lq1, lk1, lq2, lk2 = lp['diff_lambda'].astype(F32)
    lam = jnp.exp(jnp.sum(lq1 * lk1)) - jnp.exp(jnp.sum(lq2 * lk2)) + lam_init
    if latent:
        cos, sin = axial_rope(T)
        o_b = window_sink_attention(apply_rope(wq, cos, sin), apply_rope(wk, cos, sin), wv,
                                    cache[0].astype(h.dtype), cache[1].astype(h.dtype), lp['win_sink'])
        k_all = jnp.concatenate([cache[2].astype(h.dtype), apply_rope(dk, cos, sin)], axis=1)
        v_all = jnp.concatenate([cache[3].astype(h.dtype), dv], axis=1)
        o_c = diff_attention(apply_rope(dq, cos, sin), k_all, v_all, lam, lam_init, lp['diff_subln_g'])
        ctx_tensors = None
    else:
        o_b = sink_attention(wq, wk, wv, lp['win_sink'])
        o_c = diff_attention(dq, dk, dv, lam, lam_init, lp['diff_subln_g'])
        ctx_tensors = (wk, wv, dk, dv, jnp.stack([h_fwd[:, -1], h_bwd[:, 0]], axis=1))
    gates = jax.nn.sigmoid(mg.astype(F32)).astype(h.dtype).reshape(B, T, N_BRANCHES, D_MODEL)
    wb = lp['w_branch']
    merged = (gates[:, :, 0] * (o_a @ wb[:D_RNN])
              + gates[:, :, 1] * (o_b.reshape(B, T, WIN_WIDTH) @ wb[D_RNN:D_RNN + WIN_WIDTH])
              + gates[:, :, 2] * (o_c.reshape(B, T, DIFF_WIDTH) @ wb[D_RNN + WIN_WIDTH:]))
    return merged @ lp['w_out'], ctx_tensors


def grouped_experts(x, eid, wts, w_gate, w_up, w_down):
    N, D = x.shape
    A = N * TOP_K
    n_blocks = A // MOE_BLOCK + N_EXPERTS
    rows = n_blocks * MOE_BLOCK
    flat_e = eid.reshape(-1)
    flat_tok = jnp.repeat(jnp.arange(N), TOP_K)
    flat_w = wts.reshape(-1)
    order = jnp.argsort(flat_e)
    se = flat_e[order]
    counts = jnp.bincount(flat_e, length=N_EXPERTS)
    padded = (counts + MOE_BLOCK - 1) // MOE_BLOCK * MOE_BLOCK
    start = jnp.cumsum(counts) - counts
    pend = jnp.cumsum(padded)
    pstart = pend - padded
    dest = pstart[se] + jnp.arange(A) - start[se]
    slot_tok = jnp.full((rows,), N, jnp.int32).at[dest].set(flat_tok[order].astype(jnp.int32))
    slot_w = jnp.zeros((rows,), flat_w.dtype).at[dest].set(flat_w[order])
    blk_e = jnp.minimum(jnp.searchsorted(pend, jnp.arange(n_blocks) * MOE_BLOCK, side='right'), N_EXPERTS - 1)
    xp = jnp.concatenate([x, jnp.zeros((1, D), x.dtype)], axis=0)
    xs = xp[slot_tok].reshape(n_blocks, MOE_BLOCK, D)

    def expert_block(args):
        xb, e = args
        return (jax.nn.silu(xb @ w_gate[e]) * (xb @ w_up[e])) @ w_down[e]

    ys = lax.map(expert_block, (xs, blk_e)).reshape(rows, D)
    y = jnp.zeros((N + 1, D), x.dtype).at[slot_tok].add(ys * slot_w[:, None].astype(x.dtype))
    return y[:N]


def hier_moe(h, lp):
    B, T, D = h.shape
    x = h.reshape(B * T, D)
    N = x.shape[0]
    grp_p = jax.nn.softmax((x @ lp['moe_w_group']).astype(F32) + lp['moe_b_group'].astype(F32), axis=-1)
    g_p, g_i = lax.top_k(grp_p, 1)
    rt = ((x @ lp['moe_w_router']).astype(F32) + lp['moe_b_router'].astype(F32)).reshape(N, N_GROUPS, EXPERTS_PER_GROUP)
    rt = rt[jnp.arange(N), g_i[:, 0]]
    t_p, t_i = lax.top_k(jax.nn.softmax(rt, axis=-1), TOP_K)
    wts = g_p * t_p / jnp.sum(t_p, axis=-1, keepdims=True)
    eid = g_i * EXPERTS_PER_GROUP + t_i
    y = grouped_experts(x, eid, wts, lp['moe_w_gate'], lp['moe_w_up'], lp['moe_w_down'])
    return y.reshape(B, T, D)


def trunk_layer(x, cvec, lp, lam_init, cache):
    sh1, sc1, g1, sh2, sc2, g2 = ada_mod(cvec, lp['w_mod'], lp['b_mod'])
    hn = rmsnorm(x, lp['norm1_g']) * (1.0 + sc1) + sh1
    mix, ctx_tensors = token_mixers(hn, lp, lam_init, cache)
    x = x + g1 * mix
    hn = rmsnorm(x, lp['norm2_g']) * (1.0 + sc2) + sh2
    x = x + g2 * hier_moe(hn, lp)
    return x, ctx_tensors


def setup_inputs(seed: int = 0) -> dict:
    key = jax.random.key(seed)
    ks = iter(jax.random.split(key, 48))

    def nrm(shape, scale):
        return jax.random.normal(next(ks), shape, F32) * scale

    u = jax.random.uniform(next(ks), (DEPTH, 2, D_RNN), F32, minval=0.9, maxval=0.999)
    return {
        'x_prompt': nrm((BATCH, SEQ, D_MODEL), 1.0),
        'x_sample': nrm((DEC_BATCH, DEC_SEQ, D_MODEL), 1.0),
        'c': nrm((DEC_BATCH, D_MODEL), 1.0),
        'cache_win_k': nrm((DEC_BATCH, DEPTH, PAST_LEN, N_WIN_KV, HEAD_DIM), 1.0),
        'cache_win_v': nrm((DEC_BATCH, DEPTH, PAST_LEN, N_WIN_KV, HEAD_DIM), 1.0),
        'cache_diff_k': nrm((DEC_BATCH, DEPTH, PAST_LEN, N_DIFF_HEADS, 2, HEAD_DIM), 1.0),
        'cache_diff_v': nrm((DEC_BATCH, DEPTH, PAST_LEN, N_DIFF_HEADS, 2 * HEAD_DIM), 1.0),
        'state_rnn': nrm((DEC_BATCH, DEPTH, 2, D_RNN), 0.5),
        'c_ctx': nrm((D_MODEL,), 1.0),
        'w_mod': nrm((DEPTH, D_MODEL, 6 * D_MODEL), 0.5 * D_MODEL ** -0.5),
        'b_mod': nrm((DEPTH, 6 * D_MODEL), 0.02),
        'norm1_g': 1.0 + nrm((DEPTH, D_MODEL), 0.02),
        'norm2_g': 1.0 + nrm((DEPTH, D_MODEL), 0.02),
        'final_g': 1.0 + nrm((D_MODEL,), 0.02),
        'w_in': nrm((DEPTH, D_MODEL, D_IN), D_MODEL ** -0.5),
        'conv_w': nrm((DEPTH, CONV_W, D_RNN), CONV_W ** -0.5),
        'conv_b': nrm((DEPTH, D_RNN), 0.02),
        'rg_wa': nrm((DEPTH, 2, N_RNN_BLOCKS, RNN_BLOCK, RNN_BLOCK), RNN_BLOCK ** -0.5),
        'rg_ba': nrm((DEPTH, 2, D_RNN), 0.02),
        'rg_wx': nrm((DEPTH, 2, N_RNN_BLOCKS, RNN_BLOCK, RNN_BLOCK), RNN_BLOCK ** -0.5),
        'rg_bx': nrm((DEPTH, 2, D_RNN), 0.02),
        'rg_lambda': jnp.log(u) - jnp.log1p(-u),
        'win_sink': nrm((DEPTH, N_WIN_HEADS), 1.0),
        'diff_lambda': nrm((DEPTH, 4, HEAD_DIM), 0.1),
        'diff_subln_g': 1.0 + nrm((DEPTH, 2 * HEAD_DIM), 0.02),
        'w_branch': nrm((DEPTH, MIX_WIDTH, D_MODEL), (MIX_WIDTH // N_BRANCHES) ** -0.5),
        'w_out': nrm((DEPTH, D_MODEL, D_MODEL), D_MODEL ** -0.5),
        'moe_w_group': nrm((DEPTH, D_MODEL, N_GROUPS), D_MODEL ** -0.5),
        'moe_b_group': nrm((DEPTH, N_GROUPS), 0.01),
        'moe_w_router': nrm((DEPTH, D_MODEL, N_EXPERTS), D_MODEL ** -0.5),
        'moe_b_router': nrm((DEPTH, N_EXPERTS), 0.01),
        'moe_w_gate': nrm((DEPTH, N_EXPERTS, D_MODEL, D_EXPERT), D_MODEL ** -0.5),
        'moe_w_up': nrm((DEPTH, N_EXPERTS, D_MODEL, D_EXPERT), D_MODEL ** -0.5),
        'moe_w_down': nrm((DEPTH, N_EXPERTS, D_EXPERT, D_MODEL), D_EXPERT ** -0.5),
    }


def reference(x_prompt, x_sample, c, cache_win_k, cache_win_v, cache_diff_k, cache_diff_v, state_rnn,
              c_ctx, w_mod, b_mod, norm1_g, norm2_g, final_g, w_in, conv_w, conv_b, rg_wa, rg_ba,
              rg_wx, rg_bx, rg_lambda, win_sink, diff_lambda, diff_subln_g, w_branch, w_out,
              moe_w_group, moe_b_group, moe_w_router, moe_b_router, moe_w_gate, moe_w_up, moe_w_down):
    xp, xs = x_prompt, x_sample
    ctx_vec = c_ctx[None, :]
    new_wk, new_wv, new_dk, new_dv, new_st = [], [], [], [], []
    for l in range(DEPTH):
        lp = dict(w_mod=w_mod[l], b_mod=b_mod[l], norm1_g=norm1_g[l], norm2_g=norm2_g[l],
                  w_in=w_in[l], conv_w=conv_w[l], conv_b=conv_b[l], rg_wa=rg_wa[l], rg_ba=rg_ba[l],
                  rg_wx=rg_wx[l], rg_bx=rg_bx[l], rg_lambda=rg_lambda[l], win_sink=win_sink[l],
                  diff_lambda=diff_lambda[l], diff_subln_g=diff_subln_g[l], w_branch=w_branch[l],
                  w_out=w_out[l], moe_w_group=moe_w_group[l], moe_b_group=moe_b_group[l],
                  moe_w_router=moe_w_router[l], moe_b_router=moe_b_router[l],
                  moe_w_gate=moe_w_gate[l], moe_w_up=moe_w_up[l], moe_w_down=moe_w_down[l])
        lam_init = 0.8 - 0.6 * math.exp(-0.3 * l)
        xp, (wk, wv, dk, dv, st) = trunk_layer(xp, ctx_vec, lp, lam_init, None)
        new_wk.append(wk)
        new_wv.append(wv)
        new_dk.append(dk)
        new_dv.append(dv)
        new_st.append(st)
        cache_l = (cache_win_k[:, l], cache_win_v[:, l], cache_diff_k[:, l], cache_diff_v[:, l], state_rnn[:, l])
        xs, _ = trunk_layer(xs, c, lp, lam_init, cache_l)
    y_prompt = rmsnorm(xp, final_g)
    y_sample = rmsnorm(xs, final_g)
    return (y_prompt, y_sample, jnp.stack(new_wk, axis=1), jnp.stack(new_wv, axis=1),
            jnp.stack(new_dk, axis=1), jnp.stack(new_dv, axis=1), jnp.stack(new_st, axis=1))
```

```python
import functools
import math

import jax
import jax.numpy as jnp
from jax import lax
from jax.experimental import pallas as pl
from jax.experimental.pallas import tpu as pltpu

F32 = jnp.float32
BF16 = jnp.bfloat16

HEAD_DIM = 128
GRID_W = 64
ROPE_BASE = 10000.0
WINDOW = 128
CONV_LEFT = 2
RG_C = 8.0
TOP_K = 2
EPS = 1e-6
NEG_INF = -1e30

V7X_VMEM_LIMIT_BYTES = 56 * 1024 * 1024
LANES = 128
SUBLANES = 8

TOKEN_TILE = 256
MOE_ROWS = 256


def _cparams(sem):
    return pltpu.CompilerParams(dimension_semantics=sem, vmem_limit_bytes=V7X_VMEM_LIMIT_BYTES)


def _sigmoid(x):
    return 1.0 / (1.0 + jnp.exp(-x))


def _gelu_tanh(x):
    c = math.sqrt(2.0 / math.pi)
    return 0.5 * x * (1.0 + jnp.tanh(c * (x + 0.044715 * (x * x * x))))


def _largest_tile(n, candidates):
    for c in candidates:
        if n % c == 0:
            return c
    raise ValueError(f"no tile in {candidates} divides {n}")


def _cast_rows(src_ref, dst_ref, chunk):
    rows = src_ref.shape[0]
    chunk = min(chunk, rows)

    def body(k, c):
        r0 = pl.multiple_of(k * chunk, chunk)
        dst_ref[pl.ds(r0, chunk), :] = src_ref[pl.ds(r0, chunk), :].astype(dst_ref.dtype)
        return c

    lax.fori_loop(0, rows // chunk, body, 0)


def _adamod_kernel(cv_ref, w_ref, b_ref, o_ref):
    cv = cv_ref[...]
    s = cv * _sigmoid(cv)
    o_ref[...] = jnp.dot(s.astype(BF16), w_ref[...].astype(BF16),
                         preferred_element_type=F32) + b_ref[...]


def _ada_mod_all(cvecs, w_mod, b_mod):
    depth, d, n6 = w_mod.shape
    rows = cvecs.shape[0]
    tn = _largest_tile(n6, (512, 256, 128))
    return pl.pallas_call(
        _adamod_kernel,
        grid=(depth, n6 // tn),
        in_specs=[
            pl.BlockSpec((rows, d), lambda l, j: (0, 0)),
            pl.BlockSpec((None, d, tn), lambda l, j: (l, 0, j)),
            pl.BlockSpec((None, 1, tn), lambda l, j: (l, 0, j)),
        ],
        out_specs=pl.BlockSpec((None, rows, tn), lambda l, j: (l, 0, j)),
        out_shape=jax.ShapeDtypeStruct((depth, rows, n6), F32),
        compiler_params=_cparams(("arbitrary", "arbitrary")),
        name="ada_mod",
    )(cvecs, w_mod, b_mod.reshape(depth, 1, n6))


def _mod_row_map(n_prompt_tiles, tiles_per_latent_seq):
    def row(i):
        return jnp.where(i < n_prompt_tiles, 0, 1 + (i - n_prompt_tiles) // tiles_per_latent_seq)
    return row


def _norm_mod(x, g, mod_ref, shift_row, scale_row):
    ms = jnp.mean(x * x, axis=-1, keepdims=True)
    y = x * lax.rsqrt(ms + EPS) * g
    return y * (1.0 + mod_ref[scale_row:scale_row + 1, :]) + mod_ref[shift_row:shift_row + 1, :]


def _norm1_kernel(x_ref, g_ref, mod_ref, o_ref):
    o_ref[...] = _norm_mod(x_ref[...], g_ref[...], mod_ref, 0, 1).astype(o_ref.dtype)


def _norm1(x, g_all, layer, mod_l, row_map):
    n, d = x.shape
    tt = TOKEN_TILE
    return pl.pallas_call(
        _norm1_kernel,
        grid=(n // tt,),
        in_specs=[
            pl.BlockSpec((tt, d), lambda i: (i, 0)),
            pl.BlockSpec((None, 1, d), lambda i: (layer, 0, 0)),
            pl.BlockSpec((None, 6, d), lambda i: (row_map(i), 0, 0)),
        ],
        out_specs=pl.BlockSpec((tt, d), lambda i: (i, 0)),
        out_shape=jax.ShapeDtypeStruct((n, d), BF16),
        compiler_params=_cparams(("arbitrary",)),
        name="norm1_mod",
    )(x, g_all, mod_l)


def _norm2_router_kernel(x_ref, g_ref, mod_ref, wr_ref, br_ref, hn_ref, eid_ref, wts_ref, *,
                         n_groups, per_group):
    y = _norm_mod(x_ref[...], g_ref[...], mod_ref, 3, 4)
    hn_ref[...] = y
    logits = jnp.dot(y, wr_ref[...], preferred_element_type=F32,
                     precision=lax.Precision.HIGHEST) + br_ref[...]
    lane = lax.broadcasted_iota(jnp.int32, logits.shape, 1).astype(F32)
    big = float(LANES)
    gmask = lane < n_groups
    glog = jnp.where(gmask, logits, NEG_INF)
    gmax = jnp.max(glog, axis=-1, keepdims=True)
    gsum = jnp.sum(jnp.where(gmask, jnp.exp(glog - gmax), 0.0), axis=-1, keepdims=True)
    g_p = 1.0 / gsum
    g_i = jnp.min(jnp.where(gmask & (glog == gmax), lane, big), axis=-1, keepdims=True)
    lo = n_groups + g_i * per_group
    rmask = (lane >= lo) & (lane < lo + per_group)
    rlog = jnp.where(rmask, logits, NEG_INF)
    m1 = jnp.max(rlog, axis=-1, keepdims=True)
    i1 = jnp.min(jnp.where(rmask & (rlog == m1), lane, big), axis=-1, keepdims=True)
    rmask2 = rmask & (lane != i1)
    rlog2 = jnp.where(rmask2, logits, NEG_INF)
    m2 = jnp.max(rlog2, axis=-1, keepdims=True)
    i2 = jnp.min(jnp.where(rmask2 & (rlog2 == m2), lane, big), axis=-1, keepdims=True)
    e = jnp.exp(m2 - m1)
    w1 = g_p / (1.0 + e)
    w2 = w1 * e
    wts_ref[...] = jnp.where(lane == 0, w1, jnp.where(lane == 1, w2, 0.0))
    eid_ref[...] = jnp.where(lane == 0, i1 - n_groups,
                             jnp.where(lane == 1, i2 - n_groups, 0.0)).astype(jnp.int32)


def _norm2_router(x, g_all, layer, mod_l, row_map, w_rt, b_rt, n_groups, per_group):
    n, d = x.shape
    tt = TOKEN_TILE
    kern = functools.partial(_norm2_router_kernel, n_groups=n_groups, per_group=per_group)
    return pl.pallas_call(
        kern,
        grid=(n // tt,),
        in_specs=[
            pl.BlockSpec((tt, d), lambda i: (i, 0)),
            pl.BlockSpec((None, 1, d), lambda i: (layer, 0, 0)),
            pl.BlockSpec((None, 6, d), lambda i: (row_map(i), 0, 0)),
            pl.BlockSpec((d, LANES), lambda i: (0, 0)),
            pl.BlockSpec((1, LANES), lambda i: (0, 0)),
        ],
        out_specs=[
            pl.BlockSpec((tt, d), lambda i: (i, 0)),
            pl.BlockSpec((tt, LANES), lambda i: (i, 0)),
            pl.BlockSpec((tt, LANES), lambda i: (i, 0)),
        ],
        out_shape=[
            jax.ShapeDtypeStruct((n, d), F32),
            jax.ShapeDtypeStruct((n, LANES), jnp.int32),
            jax.ShapeDtypeStruct((n, LANES), F32),
        ],
        compiler_params=_cparams(("arbitrary",)),
        name="norm2_router",
    )(x, g_all, mod_l, w_rt, b_rt)


def _final_norm_kernel(x_ref, g_ref, o_ref):
    x = x_ref[...]
    ms = jnp.mean(x * x, axis=-1, keepdims=True)
    o_ref[...] = x * lax.rsqrt(ms + EPS) * g_ref[...]


def _final_norm(x, g):
    n, d = x.shape
    tt = TOKEN_TILE
    return pl.pallas_call(
        _final_norm_kernel,
        grid=(n // tt,),
        in_specs=[pl.BlockSpec((tt, d), lambda i: (i, 0)),
                  pl.BlockSpec((1, d), lambda i: (0, 0))],
        out_specs=pl.BlockSpec((tt, d), lambda i: (i, 0)),
        out_shape=jax.ShapeDtypeStruct((n, d), F32),
        compiler_params=_cparams(("arbitrary",)),
        name="final_norm",
    )(x, g.reshape(1, d))


def _mm_kernel(x_ref, w_ref, o_ref, wbf_ref):
    @pl.when(pl.program_id(1) == 0)
    def _():
        _cast_rows(w_ref, wbf_ref, 512)

    o_ref[...] = jnp.dot(x_ref[...], wbf_ref[...], preferred_element_type=F32).astype(o_ref.dtype)


def _in_proj(hn, w_in_all, layer):
    m, k = hn.shape
    n = w_in_all.shape[2]
    tn = _largest_tile(n, (1024, 512, 256))
    tm = _largest_tile(m, (512, 256))
    return pl.pallas_call(
        _mm_kernel,
        grid=(n // tn, m // tm),
        in_specs=[
            pl.BlockSpec((tm, k), lambda j, i: (i, 0)),
            pl.BlockSpec((None, k, tn), lambda j, i: (layer, 0, j), pipeline_mode=pl.Buffered(1)),
        ],
        out_specs=pl.BlockSpec((tm, tn), lambda j, i: (i, j)),
        out_shape=jax.ShapeDtypeStruct((m, n), F32),
        scratch_shapes=[pltpu.VMEM((k, tn), BF16)],
        compiler_params=_cparams(("arbitrary", "arbitrary")),
        name="in_proj",
    )(hn, w_in_all)


def _merge_kernel(oa_ref, ob_ref, oc_ref, ga_ref, gb_ref, gc_ref, wa_ref, wb_ref, wc_ref, o_ref,
                  wabf, wbbf, wcbf):
    @pl.when(pl.program_id(1) == 0)
    def _():
        _cast_rows(wa_ref, wabf, 512)
        _cast_rows(wb_ref, wbbf, 512)
        _cast_rows(wc_ref, wcbf, 512)

    def branch(o_ref_, g_ref_, w_ref_):
        gate = _sigmoid(g_ref_[...])
        return gate * jnp.dot(o_ref_[...], w_ref_[...], preferred_element_type=F32)

    acc = branch(oa_ref, ga_ref, wabf) + branch(ob_ref, gb_ref, wbbf) + branch(oc_ref, gc_ref, wcbf)
    o_ref[...] = acc.astype(o_ref.dtype)


def _merge(oa, ob, oc, proj, gate_off, w_branch_all, layer):
    m, d = oa.shape
    n = w_branch_all.shape[2]
    tn = _largest_tile(n, (256, 128))
    tm = _largest_tile(m, (512, 256))
    gb = gate_off // tn
    nb = n // tn
    wspec = lambda r: pl.BlockSpec((None, d, tn), lambda j, i: (layer, r, j),
                                   pipeline_mode=pl.Buffered(1))
    xspec = pl.BlockSpec((tm, d), lambda j, i: (i, 0))
    gspec = lambda r: pl.BlockSpec((tm, tn), lambda j, i: (i, gb + r * nb + j))
    return pl.pallas_call(
        _merge_kernel,
        grid=(nb, m // tm),
        in_specs=[xspec, xspec, xspec, gspec(0), gspec(1), gspec(2), wspec(0), wspec(1), wspec(2)],
        out_specs=pl.BlockSpec((tm, tn), lambda j, i: (i, j)),
        out_shape=jax.ShapeDtypeStruct((m, n), BF16),
        scratch_shapes=[pltpu.VMEM((d, tn), BF16)] * 3,
        compiler_params=_cparams(("arbitrary", "arbitrary")),
        name="branch_merge",
    )(oa, ob, oc, proj, proj, proj, w_branch_all, w_branch_all, w_branch_all)


def _out_proj_kernel(h_ref, w_ref, x_ref, mod_ref, o_ref, wbf_ref):
    @pl.when(pl.program_id(1) == 0)
    def _():
        _cast_rows(w_ref, wbf_ref, 512)

    mix = jnp.dot(h_ref[...], wbf_ref[...], preferred_element_type=F32)
    o_ref[...] = x_ref[...] + mod_ref[2:3, :] * mix


def _out_proj(merged, w_out_all, layer, x, mod_l, row_map):
    m, k = merged.shape
    n = w_out_all.shape[2]
    tn = _largest_tile(n, (1024, 512, 256))
    tm = TOKEN_TILE
    return pl.pallas_call(
        _out_proj_kernel,
        grid=(n // tn, m // tm),
        in_specs=[
            pl.BlockSpec((tm, k), lambda j, i: (i, 0)),
            pl.BlockSpec((None, k, tn), lambda j, i: (layer, 0, j), pipeline_mode=pl.Buffered(1)),
            pl.BlockSpec((tm, tn), lambda j, i: (i, j)),
            pl.BlockSpec((None, 6, tn), lambda j, i: (row_map(i), 0, j)),
        ],
        out_specs=pl.BlockSpec((tm, tn), lambda j, i: (i, j)),
        out_shape=jax.ShapeDtypeStruct((m, n), F32),
        scratch_shapes=[pltpu.VMEM((k, tn), BF16)],
        compiler_params=_cparams(("arbitrary", "arbitrary")),
        name="out_proj",
    )(merged, w_out_all, x, mod_l)


RG_CHUNK = 128


def _rglru_kernel(rx_ref, rg_ref, cw_ref, cb_ref, waf_ref, wxf_ref, wab_ref, wxb_ref,
                  ba_ref, bx_ref, lam_ref, h0_ref, o_ref, st_ref,
                  af, uf, ab, ub, wbf, *, seq):
    tc = RG_CHUNK
    n_chunks = seq // tc
    c = rx_ref.shape[1]
    wbf[0] = waf_ref[...].astype(BF16)
    wbf[1] = wxf_ref[...].astype(BF16)
    wbf[2] = wab_ref[...].astype(BF16)
    wbf[3] = wxb_ref[...].astype(BF16)
    cw = cw_ref[...]
    cb = cb_ref[...]

    def softplus(z):
        return jnp.maximum(z, 0.0) + jnp.log1p(jnp.exp(-jnp.abs(z)))

    sp_f = softplus(-lam_ref[0:1, :])
    sp_b = softplus(-lam_ref[1:2, :])
    row8 = lax.broadcasted_iota(jnp.int32, (tc, c), 0) & (SUBLANES - 1)

    def prep(k, carry):
        r0 = pl.multiple_of(k * tc, tc)
        cur = rx_ref[pl.ds(r0, tc), :]
        p0 = pl.multiple_of(jnp.maximum(r0 - SUBLANES, 0), SUBLANES)
        n0 = pl.multiple_of(jnp.minimum(r0 + tc, seq - SUBLANES), SUBLANES)
        prev8 = jnp.where(k > 0, rx_ref[pl.ds(p0, SUBLANES), :], 0.0)
        next8 = jnp.where(k < n_chunks - 1, rx_ref[pl.ds(n0, SUBLANES), :], 0.0)
        ext = jnp.concatenate([prev8, cur, next8], axis=0)
        ne = tc + 2 * SUBLANES
        xm2 = pltpu.roll(ext, 2, 0)[SUBLANES:SUBLANES + tc]
        xm1 = pltpu.roll(ext, 1, 0)[SUBLANES:SUBLANES + tc]
        xp1 = pltpu.roll(ext, ne - 1, 0)[SUBLANES:SUBLANES + tc]
        xc = cb + xm2 * cw[0:1] + xm1 * cw[1:2] + cur * cw[2:3] + xp1 * cw[3:4]
        xcb = xc.astype(BF16)

        def gates(wi, d, sp):
            r = _sigmoid(jnp.dot(xcb, wbf[wi], preferred_element_type=F32) + ba_ref[d:d + 1, :])
            i = _sigmoid(jnp.dot(xcb, wbf[wi + 1], preferred_element_type=F32) + bx_ref[d:d + 1, :])
            log_a = (-RG_C) * r * sp
            a = jnp.exp(log_a)
            u = jnp.sqrt(1.0 - a * a) * i * xc
            return a, u

        a, u = gates(0, 0, sp_f)
        for dd in (1, 2, 4):
            msk = row8 >= dd
            a_s = jnp.where(msk, pltpu.roll(a, dd, 0), 1.0)
            u_s = jnp.where(msk, pltpu.roll(u, dd, 0), 0.0)
            u = a * u_s + u
            a = a * a_s
        af[pl.ds(r0, tc), :] = a
        uf[pl.ds(r0, tc), :] = u

        a, u = gates(2, 1, sp_b)
        for dd in (1, 2, 4):
            msk = row8 < SUBLANES - dd
            a_s = jnp.where(msk, pltpu.roll(a, tc - dd, 0), 1.0)
            u_s = jnp.where(msk, pltpu.roll(u, tc - dd, 0), 0.0)
            u = a * u_s + u
            a = a * a_s
        ab[pl.ds(r0, tc), :] = a
        ub[pl.ds(r0, tc), :] = u
        return carry

    lax.fori_loop(0, n_chunks, prep, 0)

    n8 = seq // SUBLANES

    def carry_step(k, carry):
        hf, hb = carry
        i0 = pl.multiple_of(k * SUBLANES, SUBLANES)
        hf_t = af[pl.ds(i0, SUBLANES), :] * hf + uf[pl.ds(i0, SUBLANES), :]
        uf[pl.ds(i0, SUBLANES), :] = hf_t
        j0 = pl.multiple_of((n8 - 1 - k) * SUBLANES, SUBLANES)
        hb_t = ab[pl.ds(j0, SUBLANES), :] * hb + ub[pl.ds(j0, SUBLANES), :]
        ub[pl.ds(j0, SUBLANES), :] = hb_t
        return hf_t[SUBLANES - 1:SUBLANES, :], hb_t[0:1, :]

    hf, hb = lax.fori_loop(0, n8, carry_step, (h0_ref[0:1, :], h0_ref[1:2, :]))
    st_ref[0:1, :] = hf
    st_ref[1:2, :] = hb

    def finish(k, carry):
        r0 = pl.multiple_of(k * tc, tc)
        hsum = uf[pl.ds(r0, tc), :] + ub[pl.ds(r0, tc), :]
        o_ref[pl.ds(r0, tc), :] = (hsum * _gelu_tanh(rg_ref[pl.ds(r0, tc), :])).astype(o_ref.dtype)
        return carry

    lax.fori_loop(0, n_chunks, finish, 0)


def _rglru(proj, row_blk0, n_seq, seq, d, conv_w, conv_b, rg_wa, rg_wx, rg_ba, rg_bx, rg_lambda,
           h0, layer):
    nb, c = rg_wa.shape[2], rg_wa.shape[3]
    kern = functools.partial(_rglru_kernel, seq=seq)
    wspec = lambda direction: pl.BlockSpec((None, None, None, c, c),
                                           lambda b, n: (layer, direction, n, 0, 0))
    vec2 = pl.BlockSpec((None, 2, c), lambda b, n: (layer, 0, n))
    return pl.pallas_call(
        kern,
        grid=(n_seq, nb),
        in_specs=[
            pl.BlockSpec((seq, c), lambda b, n: (row_blk0 + b, n)),
            pl.BlockSpec((seq, c), lambda b, n: (row_blk0 + b, nb + n)),
            pl.BlockSpec((None, conv_w.shape[1], c), lambda b, n: (layer, 0, n)),
            pl.BlockSpec((None, 1, c), lambda b, n: (layer, 0, n)),
            wspec(0), wspec(0), wspec(1), wspec(1),
            vec2, vec2, vec2,
            pl.BlockSpec((None, 2, c), lambda b, n: (b, 0, n)),
        ],
        out_specs=[
            pl.BlockSpec((seq, c), lambda b, n: (b, n)),
            pl.BlockSpec((None, 2, c), lambda b, n: (b, 0, n)),
        ],
        out_shape=[
            jax.ShapeDtypeStruct((n_seq * seq, d), BF16),
            jax.ShapeDtypeStruct((n_seq, 2, d), F32),
        ],
        scratch_shapes=[pltpu.VMEM((seq, c), F32)] * 4 + [pltpu.VMEM((4, c, c), BF16)],
        compiler_params=_cparams(("arbitrary", "arbitrary")),
        name="rglru",
    )(proj, proj, conv_w, conv_b, rg_wa, rg_wx, rg_wa, rg_wx, rg_ba, rg_bx, rg_lambda, h0)


def _rope(x, cos_f, sin_f):
    return x * cos_f + pltpu.roll(x, HEAD_DIM // 2, 1) * sin_f


def _sink_softmax_pv(s, sink_col, v):
    m = jnp.maximum(jnp.max(s, axis=-1, keepdims=True), sink_col)
    p = jnp.exp(s - m)
    denom = jnp.sum(p, axis=-1, keepdims=True) + jnp.exp(sink_col - m)
    o = jnp.dot(p.astype(BF16), v, preferred_element_type=F32)
    return o / denom


def _sink_col(sink_ref, layer, hkv, group, rows_per_head):
    rows = group * rows_per_head
    assert rows_per_head & (rows_per_head - 1) == 0
    head = lax.broadcasted_iota(jnp.int32, (rows, 1), 0) >> (rows_per_head.bit_length() - 1)
    col = jnp.zeros((rows, 1), F32)
    for g in range(group):
        col = jnp.where(head == g, sink_ref[layer, hkv * group + g], col)
    return col


def _win_ctx_kernel(sink_ref, q_ref, k_ref, v_ref, o_ref, *, layer, group):
    hkv = pl.program_id(1)
    tq = q_ref.shape[0]
    q = jnp.concatenate([q_ref[:, g * HEAD_DIM:(g + 1) * HEAD_DIM] for g in range(group)], axis=0)
    s = lax.dot_general(q.astype(BF16), k_ref[...].astype(BF16), (((1,), (1,)), ((), ())),
                        preferred_element_type=F32) * (HEAD_DIM ** -0.5)
    o = _sink_softmax_pv(s, _sink_col(sink_ref, layer, hkv, group, tq), v_ref[...].astype(BF16))
    for g in range(group):
        o_ref[:, g * HEAD_DIM:(g + 1) * HEAD_DIM] = o[g * tq:(g + 1) * tq].astype(o_ref.dtype)


def _win_ctx(proj, n_seq, seq, q_off, k_off, v_off, n_kv, group, win_sink, layer):
    gw = group * HEAD_DIM
    kern = functools.partial(_win_ctx_kernel, layer=layer, group=group)
    return pl.pallas_call(
        kern,
        grid_spec=pltpu.PrefetchScalarGridSpec(
            num_scalar_prefetch=1,
            grid=(n_seq, n_kv),
            in_specs=[
                pl.BlockSpec((seq, gw), lambda b, h, s: (b, q_off // gw + h)),
                pl.BlockSpec((seq, HEAD_DIM), lambda b, h, s: (b, k_off // HEAD_DIM + h)),
                pl.BlockSpec((seq, HEAD_DIM), lambda b, h, s: (b, v_off // HEAD_DIM + h)),
            ],
            out_specs=pl.BlockSpec((seq, gw), lambda b, h, s: (b, h)),
        ),
        out_shape=jax.ShapeDtypeStruct((n_seq * seq, n_kv * gw), BF16),
        compiler_params=_cparams(("arbitrary", "arbitrary")),
        name="win_attn_ctx",
    )(win_sink, proj, proj, proj)


def _win_lat_kernel(sink_ref, q_ref, kp_ref, kc_ref, kn_ref, vp_ref, vc_ref, vn_ref, ck_ref, cv_ref,
                    cq_ref, sq_ref, cp_ref, sp_ref, cc_ref, sc_ref, cn_ref, sn_ref, o_ref, *,
                    layer, group, n_qblk):
    hkv = pl.program_id(1)
    qi = pl.program_id(2)
    tq = q_ref.shape[0]
    past = ck_ref.shape[0]
    cq, sq = cq_ref[...], sq_ref[...]
    q = jnp.concatenate([_rope(q_ref[:, g * HEAD_DIM:(g + 1) * HEAD_DIM], cq, sq)
                         for g in range(group)], axis=0).astype(BF16)
    k = jnp.concatenate([
        ck_ref[...].astype(BF16),
        _rope(kp_ref[...], cp_ref[...], sp_ref[...]).astype(BF16),
        _rope(kc_ref[...], cc_ref[...], sc_ref[...]).astype(BF16),
        _rope(kn_ref[...], cn_ref[...], sn_ref[...]).astype(BF16)], axis=0)
    v = jnp.concatenate([cv_ref[...].astype(BF16), vp_ref[...].astype(BF16),
                         vc_ref[...].astype(BF16), vn_ref[...].astype(BF16)], axis=0)
    s = lax.dot_general(q, k, (((1,), (1,)), ((), ())),
                        preferred_element_type=F32) * (HEAD_DIM ** -0.5)
    r = lax.broadcasted_iota(jnp.int32, s.shape, 0) & (tq - 1)
    c = lax.broadcasted_iota(jnp.int32, s.shape, 1) - past
    in_prev = (c >= 0) & (c < tq)
    in_next = c >= 2 * tq
    bad = (in_prev & ((c < r + (tq - WINDOW)) | (qi == 0))) | \
          (in_next & ((c - 2 * tq > r - (tq - WINDOW)) | (qi == n_qblk - 1)))
    s = jnp.where(bad, NEG_INF, s)
    o = _sink_softmax_pv(s, _sink_col(sink_ref, layer, hkv, group, tq), v)
    for g in range(group):
        o_ref[:, g * HEAD_DIM:(g + 1) * HEAD_DIM] = o[g * tq:(g + 1) * tq].astype(o_ref.dtype)


def _win_lat(proj, row0, n_seq, seq, q_off, k_off, v_off, n_kv, group, win_sink, layer,
             cache_k, cache_v, cos_f, sin_f):
    tq = WINDOW
    assert seq % tq == 0 and row0 % tq == 0
    nq = seq // tq
    gw = group * HEAD_DIM
    past = cache_k.shape[2]
    rb0 = row0 // tq
    kb, vb = k_off // HEAD_DIM, v_off // HEAD_DIM
    kern = functools.partial(_win_lat_kernel, layer=layer, group=group, n_qblk=nq)
    prv = lambda i: jnp.maximum(i - 1, 0)
    nxt = lambda i: jnp.minimum(i + 1, nq - 1)
    kv = lambda colb, f: pl.BlockSpec((tq, HEAD_DIM),
                                      lambda b, h, i, s: (rb0 + b * nq + f(i), colb + h))
    cache = pl.BlockSpec((None, None, past, HEAD_DIM), lambda b, h, i, s: (b, layer, 0, h))
    tab = lambda f: pl.BlockSpec((tq, HEAD_DIM), lambda b, h, i, s: (f(i), 0))
    same = lambda i: i
    return pl.pallas_call(
        kern,
        grid_spec=pltpu.PrefetchScalarGridSpec(
            num_scalar_prefetch=1,
            grid=(n_seq, n_kv, nq),
            in_specs=[
                pl.BlockSpec((tq, gw), lambda b, h, i, s: (rb0 + b * nq + i, q_off // gw + h)),
                kv(kb, prv), kv(kb, same), kv(kb, nxt),
                kv(vb, prv), kv(vb, same), kv(vb, nxt),
                cache, cache,
                tab(same), tab(same), tab(prv), tab(prv), tab(same), tab(same), tab(nxt), tab(nxt),
            ],
            out_specs=pl.BlockSpec((tq, gw), lambda b, h, i, s: (b * nq + i, h)),
        ),
        out_shape=jax.ShapeDtypeStruct((n_seq * seq, n_kv * gw), BF16),
        compiler_params=_cparams(("arbitrary", "arbitrary", "arbitrary")),
        name="win_attn_lat",
    )(win_sink, proj, proj, proj, proj, proj, proj, proj, cache_k, cache_v,
      cos_f, sin_f, cos_f, sin_f, cos_f, sin_f, cos_f, sin_f)


def _diff_lambda(l4, lam_init):
    t1 = jnp.sum(l4[0:1, :] * l4[1:2, :], axis=-1, keepdims=True)
    t2 = jnp.sum(l4[2:3, :] * l4[3:4, :], axis=-1, keepdims=True)
    return jnp.exp(t1) - jnp.exp(t2) + lam_init


def _softmax_rows(s):
    m = jnp.max(s, axis=-1, keepdims=True)
    p = jnp.exp(s - m)
    return p / jnp.sum(p, axis=-1, keepdims=True)


def _diff_core(q0, q1, k0, k1, v, lam, g, lam_init):
    dn = (((1,), (1,)), ((), ()))
    scale = HEAD_DIM ** -0.5
    p0 = _softmax_rows(lax.dot_general(q0, k0, dn, preferred_element_type=F32) * scale)
    p1 = _softmax_rows(lax.dot_general(q1, k1, dn, preferred_element_type=F32) * scale)
    a = p0 - lam * p1
    o = jnp.dot(a.astype(BF16), v, preferred_element_type=F32)
    ms = jnp.mean(o * o, axis=-1, keepdims=True)
    return o * lax.rsqrt(ms + EPS) * g * (1.0 - lam_init)


def _diff_ctx_kernel(q_ref, k_ref, v_ref, l4_ref, g_ref, o_ref, *, lam_init):
    lam = _diff_lambda(l4_ref[...], lam_init)
    hd = HEAD_DIM
    o = _diff_core(q_ref[:, 0:hd].astype(BF16), q_ref[:, hd:2 * hd].astype(BF16),
                   k_ref[:, 0:hd].astype(BF16), k_ref[:, hd:2 * hd].astype(BF16),
                   v_ref[...].astype(BF16), lam, g_ref[...], lam_init)
    o_ref[...] = o.astype(o_ref.dtype)


def _diff_ctx(proj, n_seq, seq, q_off, k_off, v_off, n_heads, diff_lambda, subln_g, layer, lam_init):
    hw = 2 * HEAD_DIM
    kern = functools.partial(_diff_ctx_kernel, lam_init=lam_init)
    blk = lambda off: pl.BlockSpec((seq, hw), lambda b, h: (b, off // hw + h))
    return pl.pallas_call(
        kern,
        grid=(n_seq, n_heads),
        in_specs=[blk(q_off), blk(k_off), blk(v_off),
                  pl.BlockSpec((None, 4, HEAD_DIM), lambda b, h: (layer, 0, 0)),
                  pl.BlockSpec((None, 1, hw), lambda b, h: (layer, 0, 0))],
        out_specs=pl.BlockSpec((seq, hw), lambda b, h: (b, h)),
        out_shape=jax.ShapeDtypeStruct((n_seq * seq, n_heads * hw), BF16),
        compiler_params=_cparams(("arbitrary", "arbitrary")),
        name="diff_attn_ctx",
    )(proj, proj, proj, diff_lambda, subln_g)


DIFF_KV_CHUNK = 256


def _diff_lat_kernel(q_ref, k_ref, v_ref, ck_ref, cv_ref, cq_ref, sq_ref, call_ref, sall_ref,
                     l4_ref, g_ref, o_ref, kall, vall, *, lam_init):
    hd = HEAD_DIM
    past = ck_ref.shape[0]
    seq = k_ref.shape[0]

    @pl.when(pl.program_id(2) == 0)
    def _():
        kall[0:past, :] = ck_ref[...].astype(BF16)
        vall[0:past, :] = cv_ref[...].astype(BF16)

        def fill(i, carry):
            r0 = pl.multiple_of(i * DIFF_KV_CHUNK, DIFF_KV_CHUNK)
            rows = pl.ds(r0, DIFF_KV_CHUNK)
            dst = pl.ds(past + r0, DIFF_KV_CHUNK)
            cs, sn = call_ref[rows, :], sall_ref[rows, :]
            kall[dst, 0:hd] = _rope(k_ref[rows, 0:hd], cs, sn).astype(BF16)
            kall[dst, hd:2 * hd] = _rope(k_ref[rows, hd:2 * hd], cs, sn).astype(BF16)
            vall[dst, :] = v_ref[rows, :].astype(BF16)
            return carry

        lax.fori_loop(0, seq // DIFF_KV_CHUNK, fill, 0)

    lam = _diff_lambda(l4_ref[...], lam_init)
    cq, sq = cq_ref[...], sq_ref[...]
    o = _diff_core(_rope(q_ref[:, 0:hd], cq, sq).astype(BF16),
                   _rope(q_ref[:, hd:2 * hd], cq, sq).astype(BF16),
                   kall[:, 0:hd], kall[:, hd:2 * hd], vall[...], lam, g_ref[...], lam_init)
    o_ref[...] = o.astype(o_ref.dtype)


def _diff_lat(proj, row0, n_seq, seq, q_off, k_off, v_off, n_heads, diff_lambda, subln_g, layer,
              lam_init, cache_k, cache_v, cos_f, sin_f):
    hw = 2 * HEAD_DIM
    tq = 128
    assert seq % DIFF_KV_CHUNK == 0 and row0 % seq == 0
    nq = seq // tq
    past = cache_k.shape[2]
    sb0 = row0 // seq
    kern = functools.partial(_diff_lat_kernel, lam_init=lam_init)
    full = lambda off: pl.BlockSpec((seq, hw), lambda b, h, i: (sb0 + b, off // hw + h))
    cache = pl.BlockSpec((None, None, past, hw), lambda b, h, i: (b, layer, 0, h))
    return pl.pallas_call(
        kern,
        grid=(n_seq, n_heads, nq),
        in_specs=[
            pl.BlockSpec((tq, hw), lambda b, h, i: (sb0 * nq + b * nq + i, q_off // hw + h)),
            full(k_off), full(v_off), cache, cache,
            pl.BlockSpec((tq, HEAD_DIM), lambda b, h, i: (i, 0)),
            pl.BlockSpec((tq, HEAD_DIM), lambda b, h, i: (i, 0)),
            pl.BlockSpec((seq, HEAD_DIM), lambda b, h, i: (0, 0)),
            pl.BlockSpec((seq, HEAD_DIM), lambda b, h, i: (0, 0)),
            pl.BlockSpec((None, 4, HEAD_DIM), lambda b, h, i: (layer, 0, 0)),
            pl.BlockSpec((None, 1, hw), lambda b, h, i: (layer, 0, 0)),
        ],
        out_specs=pl.BlockSpec((tq, hw), lambda b, h, i: (b * nq + i, h)),
        out_shape=jax.ShapeDtypeStruct((n_seq * seq, n_heads * hw), BF16),
        scratch_shapes=[pltpu.VMEM((past + seq, hw), BF16), pltpu.VMEM((past + seq, hw), BF16)],
        compiler_params=_cparams(("arbitrary", "arbitrary", "arbitrary")),
        name="diff_attn_lat",
    )(proj, proj, proj, cache_k, cache_v, cos_f, sin_f, cos_f, sin_f, diff_lambda, subln_g)


def _moe_plan(eid, n_experts, rows_per_blk):
    n = eid.shape[0]
    a = n * TOP_K
    flat_e = eid.reshape(-1)
    onehot = (flat_e[:, None] == jnp.arange(n_experts, dtype=jnp.int32)[None, :]).astype(jnp.int32)
    csum = jnp.cumsum(onehot, axis=0)
    rank = jnp.take_along_axis(csum, flat_e[:, None], axis=1)[:, 0] - 1
    counts = csum[-1]
    nblk_e = (counts + rows_per_blk - 1) // rows_per_blk
    blk_end = jnp.cumsum(nblk_e)
    blk_start = blk_end - nblk_e
    dest = (blk_start[flat_e] * rows_per_blk + rank).astype(jnp.int32)
    n_blk = a // rows_per_blk + n_experts
    n_used = blk_end[-1].astype(jnp.int32)
    slot_tok = jnp.zeros((n_blk * rows_per_blk,), jnp.int32).at[dest].set(
        jnp.arange(a, dtype=jnp.int32) // TOP_K)
    blk_ids = jnp.arange(n_blk, dtype=jnp.int32)
    blk_e = jnp.minimum(jnp.searchsorted(blk_end, blk_ids, side='right'), n_experts - 1)
    blk_e = jnp.where(blk_ids < n_used, blk_e, blk_e[jnp.maximum(n_used - 1, 0)]).astype(jnp.int32)
    first = jnp.concatenate([jnp.ones((1,), jnp.int32),
                             (blk_e[1:] != blk_e[:-1]).astype(jnp.int32)])
    return slot_tok, dest, blk_e, first, n_used.reshape(1)


def _gather_rows_kernel(tok_ref, x_hbm, o_hbm, sem, *, rows_per_step):
    base = pl.program_id(0) * rows_per_step

    def issue(r, carry):
        tok = tok_ref[base + r]
        pltpu.make_async_copy(x_hbm.at[pl.ds(tok, 1)], o_hbm.at[pl.ds(base + r, 1)], sem).start()
        return carry

    lax.fori_loop(0, rows_per_step, issue, 0)
    pltpu.make_async_copy(x_hbm.at[pl.ds(0, rows_per_step)],
                          o_hbm.at[pl.ds(base, rows_per_step)], sem).wait()


def _gather_rows(x, slot_tok):
    rows = slot_tok.shape[0]
    d = x.shape[1]
    rps = MOE_ROWS
    kern = functools.partial(_gather_rows_kernel, rows_per_step=rps)
    return pl.pallas_call(
        kern,
        grid_spec=pltpu.PrefetchScalarGridSpec(
            num_scalar_prefetch=1,
            grid=(rows // rps,),
            in_specs=[pl.BlockSpec(memory_space=pl.ANY)],
            out_specs=pl.BlockSpec(memory_space=pl.ANY),
            scratch_shapes=[pltpu.SemaphoreType.DMA(())],
        ),
        out_shape=jax.ShapeDtypeStruct((rows, d), x.dtype),
        compiler_params=_cparams(("arbitrary",)),
        name="moe_gather",
    )(slot_tok, x)


def _gmm1_kernel(be_ref, first_ref, nused_ref, x_ref, wg_ref, wu_ref, o_ref, wgb, wub):
    blk = pl.program_id(1)

    @pl.when(first_ref[blk] == 1)
    def _():
        _cast_rows(wg_ref, wgb, 512)
        _cast_rows(wu_ref, wub, 512)

    @pl.when(blk < nused_ref[0])
    def _():
        x = x_ref[...].astype(BF16)
        g = jnp.dot(x, wgb[...], preferred_element_type=F32)
        u = jnp.dot(x, wub[...], preferred_element_type=F32)
        o_ref[...] = (g * _sigmoid(g) * u).astype(o_ref.dtype)

    @pl.when(blk >= nused_ref[0])
    def _():
        o_ref[...] = jnp.zeros(o_ref.shape, o_ref.dtype)


def _gmm1(xs, w_gate_all, w_up_all, layer, blk_e, first, n_used):
    rows, d = xs.shape
    de = w_gate_all.shape[3]
    tn = _largest_tile(de, (512, 256, 128))
    r = MOE_ROWS
    wspec = pl.BlockSpec((None, None, d, tn), lambda j, b, be, fi, nu: (layer, be[b], 0, j))
    return pl.pallas_call(
        _gmm1_kernel,
        grid_spec=pltpu.PrefetchScalarGridSpec(
            num_scalar_prefetch=3,
            grid=(de // tn, rows // r),
            in_specs=[pl.BlockSpec((r, d), lambda j, b, be, fi, nu: (b, 0)), wspec, wspec],
            out_specs=pl.BlockSpec((r, tn), lambda j, b, be, fi, nu: (b, j)),
            scratch_shapes=[pltpu.VMEM((d, tn), BF16), pltpu.VMEM((d, tn), BF16)],
        ),
        out_shape=jax.ShapeDtypeStruct((rows, de), BF16),
        compiler_params=_cparams(("arbitrary", "arbitrary")),
        name="moe_gate_up",
    )(blk_e, first, n_used, xs, w_gate_all, w_up_all)


def _gmm2_kernel(be_ref, first_ref, nused_ref, h_ref, wd_ref, o_ref, wdb):
    blk = pl.program_id(1)

    @pl.when(first_ref[blk] == 1)
    def _():
        _cast_rows(wd_ref, wdb, 512)

    @pl.when(blk < nused_ref[0])
    def _():
        o_ref[...] = jnp.dot(h_ref[...], wdb[...], preferred_element_type=F32)

    @pl.when(blk >= nused_ref[0])
    def _():
        o_ref[...] = jnp.zeros(o_ref.shape, o_ref.dtype)


def _gmm2(h, w_down_all, layer, blk_e, first, n_used):
    rows, de = h.shape
    d = w_down_all.shape[3]
    tn = _largest_tile(d, (1024, 512, 256))
    r = MOE_ROWS
    return pl.pallas_call(
        _gmm2_kernel,
        grid_spec=pltpu.PrefetchScalarGridSpec(
            num_scalar_prefetch=3,
            grid=(d // tn, rows // r),
            in_specs=[
                pl.BlockSpec((r, de), lambda j, b, be, fi, nu: (b, 0)),
                pl.BlockSpec((None, None, de, tn), lambda j, b, be, fi, nu: (layer, be[b], 0, j)),
            ],
            out_specs=pl.BlockSpec((r, tn), lambda j, b, be, fi, nu: (b, j)),
            scratch_shapes=[pltpu.VMEM((de, tn), BF16)],
        ),
        out_shape=jax.ShapeDtypeStruct((rows, d), F32),
        compiler_params=_cparams(("arbitrary", "arbitrary")),
        name="moe_down",
    )(blk_e, first, n_used, h, w_down_all)


def _combine_kernel(slot_ref, x_ref, wts_ref, mod_ref, ys_hbm, o_ref, buf, sem):
    tt = x_ref.shape[0]
    base = pl.program_id(0) * tt

    def issue(r, carry):
        for k in range(TOP_K):
            s = slot_ref[TOP_K * (base + r) + k]
            pltpu.make_async_copy(ys_hbm.at[pl.ds(s, 1)], buf.at[k, pl.ds(r, 1)], sem).start()
        return carry

    lax.fori_loop(0, tt, issue, 0)
    for k in range(TOP_K):
        pltpu.make_async_copy(ys_hbm.at[pl.ds(0, tt)], buf.at[k], sem).wait()
    w = wts_ref[...]
    y = w[:, 0:1] * buf[0] + w[:, 1:2] * buf[1]
    o_ref[...] = x_ref[...] + mod_ref[5:6, :] * y


def _combine(x, wts, mod_l, row_map, ys, slots):
    n, d = x.shape
    tt = TOKEN_TILE
    return pl.pallas_call(
        _combine_kernel,
        grid_spec=pltpu.PrefetchScalarGridSpec(
            num_scalar_prefetch=1,
            grid=(n // tt,),
            in_specs=[
                pl.BlockSpec((tt, d), lambda i, s: (i, 0)),
                pl.BlockSpec((tt, LANES), lambda i, s: (i, 0)),
                pl.BlockSpec((None, 6, d), lambda i, s: (row_map(i), 0, 0)),
                pl.BlockSpec(memory_space=pl.ANY),
            ],
            out_specs=pl.BlockSpec((tt, d), lambda i, s: (i, 0)),
            scratch_shapes=[pltpu.VMEM((TOP_K, tt, d), F32), pltpu.SemaphoreType.DMA(())],
        ),
        out_shape=jax.ShapeDtypeStruct((n, d), F32),
        compiler_params=_cparams(("arbitrary",)),
        name="moe_combine",
    )(slots, x, wts, mod_l, ys)


def _rope_tables(seq):
    rows = seq // GRID_W
    row = jnp.repeat(jnp.arange(rows), GRID_W).astype(F32)
    col = jnp.tile(jnp.arange(GRID_W), rows).astype(F32)
    n_freq = HEAD_DIM // 4
    inv = 1.0 / (ROPE_BASE ** (jnp.arange(n_freq, dtype=F32) / n_freq))
    ang = jnp.concatenate([row[:, None] * inv, col[:, None] * inv], axis=-1)
    cos, sin = jnp.cos(ang), jnp.sin(ang)
    return jnp.concatenate([cos, cos], axis=-1), jnp.concatenate([-sin, sin], axis=-1)


def kernel(x_prompt, x_sample, c, cache_win_k, cache_win_v, cache_diff_k, cache_diff_v, state_rnn, c_ctx, w_mod, b_mod, norm1_g, norm2_g, final_g, w_in, conv_w, conv_b, rg_wa, rg_ba, rg_wx, rg_bx, rg_lambda, win_sink, diff_lambda, diff_subln_g, w_branch, w_out, moe_w_group, moe_b_group, moe_w_router, moe_b_router, moe_w_gate, moe_w_up, moe_w_down):
    bp, tp, d = x_prompt.shape
    bs, ts, _ = x_sample.shape
    depth = w_in.shape[0]
    past = cache_win_k.shape[2]
    n_kv = cache_win_k.shape[3]
    n_win = win_sink.shape[1]
    group = n_win // n_kv
    n_diff = cache_diff_k.shape[3]
    d_rnn = rg_ba.shape[2]
    n_groups = moe_w_group.shape[2]
    n_experts = moe_w_router.shape[2]
    per_group = n_experts // n_groups
    n_p, n_s = bp * tp, bs * ts
    tt = TOKEN_TILE
    assert tp % tt == 0 and ts % tt == 0 and n_p % ts == 0 and d_rnn == d
    assert n_groups + n_experts <= LANES

    sections = (d_rnn, d_rnn, n_win * HEAD_DIM, n_kv * HEAD_DIM, n_kv * HEAD_DIM,
                n_diff * 2 * HEAD_DIM, n_diff * 2 * HEAD_DIM, n_diff * 2 * HEAD_DIM, 3 * d)
    offs = [0]
    for s_ in sections:
        offs.append(offs[-1] + s_)
    _, _, o_wq, o_wk, o_wv, o_dq, o_dk, o_dv, o_mg = offs[:9]

    row_map = _mod_row_map(n_p // tt, ts // tt)

    n_rows = -(-(1 + bs) // SUBLANES) * SUBLANES
    cvecs = jnp.zeros((n_rows, d), F32).at[0].set(c_ctx).at[1:1 + bs].set(c)
    mod = _ada_mod_all(cvecs, w_mod, b_mod).reshape(depth, n_rows, 6, d)

    cos_f, sin_f = _rope_tables(ts)
    ck_win = cache_win_k.reshape(bs, depth, past, n_kv * HEAD_DIM)
    cv_win = cache_win_v.reshape(bs, depth, past, n_kv * HEAD_DIM)
    ck_diff = cache_diff_k.reshape(bs, depth, past, n_diff * 2 * HEAD_DIM)
    cv_diff = cache_diff_v.reshape(bs, depth, past, n_diff * 2 * HEAD_DIM)
    w_rt = jnp.zeros((depth, d, LANES), F32).at[:, :, :n_groups].set(moe_w_group) \
        .at[:, :, n_groups:n_groups + n_experts].set(moe_w_router)
    b_rt = jnp.zeros((depth, 1, LANES), F32).at[:, 0, :n_groups].set(moe_b_group) \
        .at[:, 0, n_groups:n_groups + n_experts].set(moe_b_router)
    h0_prompt = jnp.zeros((bp, 2, d_rnn), F32)
    norm1_g = norm1_g.reshape(depth, 1, d)
    norm2_g = norm2_g.reshape(depth, 1, d)
    conv_b = conv_b.reshape(depth, 1, d_rnn)
    diff_subln_g = diff_subln_g.reshape(depth, 1, 2 * HEAD_DIM)

    x = jnp.concatenate([x_prompt.reshape(n_p, d), x_sample.reshape(n_s, d)], axis=0)
    new_wk, new_wv, new_dk, new_dv, new_st = [], [], [], [], []
    for l in range(depth):
        lam_init = 0.8 - 0.6 * math.exp(-0.3 * l)
        mod_l = mod[l]
        hn = _norm1(x, norm1_g, l, mod_l, row_map)
        proj = _in_proj(hn, w_in, l)

        rg_args = (conv_w, conv_b, rg_wa, rg_wx, rg_ba, rg_bx, rg_lambda)
        oa_p, st_p = _rglru(proj, 0, bp, tp, d_rnn, *rg_args, h0_prompt, l)
        oa_s, _ = _rglru(proj, n_p // ts, bs, ts, d_rnn, *rg_args, state_rnn[:, l], l)

        ob_p = _win_ctx(proj, bp, tp, o_wq, o_wk, o_wv, n_kv, group, win_sink, l)
        ob_s = _win_lat(proj, n_p, bs, ts, o_wq, o_wk, o_wv, n_kv, group, win_sink, l,
                        ck_win, cv_win, cos_f, sin_f)
        oc_p = _diff_ctx(proj, bp, tp, o_dq, o_dk, o_dv, n_diff, diff_lambda, diff_subln_g, l, lam_init)
        oc_s = _diff_lat(proj, n_p, bs, ts, o_dq, o_dk, o_dv, n_diff, diff_lambda, diff_subln_g, l,
                         lam_init, ck_diff, cv_diff, cos_f, sin_f)

        oa = jnp.concatenate([oa_p, oa_s], axis=0)
        ob = jnp.concatenate([ob_p, ob_s], axis=0)
        oc = jnp.concatenate([oc_p, oc_s], axis=0)
        merged = _merge(oa, ob, oc, proj, o_mg, w_branch, l)
        x = _out_proj(merged, w_out, l, x, mod_l, row_map)

        hn2, eid, wts = _norm2_router(x, norm2_g, l, mod_l, row_map, w_rt[l], b_rt[l],
                                      n_groups, per_group)
        slot_tok, dest, blk_e, first, n_used = _moe_plan(eid[:, :TOP_K], n_experts, MOE_ROWS)
        xs = _gather_rows(hn2, slot_tok)
        hmid = _gmm1(xs, moe_w_gate, moe_w_up, l, blk_e, first, n_used)
        ys = _gmm2(hmid, moe_w_down, l, blk_e, first, n_used)
        x = _combine(x, wts, mod_l, row_map, ys, dest)

        pr = proj[:n_p]
        new_wk.append(pr[:, o_wk:o_wv].reshape(bp, tp, n_kv, HEAD_DIM))
        new_wv.append(pr[:, o_wv:o_dq].reshape(bp, tp, n_kv, HEAD_DIM))
        new_dk.append(pr[:, o_dk:o_dv].reshape(bp, tp, n_diff, 2, HEAD_DIM))
        new_dv.append(pr[:, o_dv:o_mg].reshape(bp, tp, n_diff, 2 * HEAD_DIM))
        new_st.append(st_p)

    y = _final_norm(x, final_g)
    return (y[:n_p].reshape(bp, tp, d), y[n_p:].reshape(bs, ts, d),
            jnp.stack(new_wk, axis=1), jnp.stack(new_wv, axis=1),
            jnp.stack(new_dk, axis=1), jnp.stack(new_dv, axis=1), jnp.stack(new_st, axis=1))
```

```python
import functools
import math

import jax
import jax.numpy as jnp
from jax import lax
from jax.experimental import pallas as pl
from jax.experimental.pallas import tpu as pltpu

F32 = jnp.float32
BF16 = jnp.bfloat16

HEAD_DIM = 128
GRID_W = 64
ROPE_BASE = 10000.0
WINDOW = 128
CONV_LEFT = 2
RG_C = 8.0
TOP_K = 2
EPS = 1e-6
NEG_INF = -1e30

V7X_VMEM_LIMIT_BYTES = 56 * 1024 * 1024
LANES = 128
SUBLANES = 8

TOKEN_TILE = 256
MOE_ROWS = 256


def _cparams(sem):
    return pltpu.CompilerParams(dimension_semantics=sem, vmem_limit_bytes=V7X_VMEM_LIMIT_BYTES)


def _sigmoid(x):
    return 0.5 * jnp.tanh(0.5 * x) + 0.5


def _roll8(x, shift):
    r, c = x.shape
    return pltpu.roll(x.reshape(r // SUBLANES, SUBLANES, c), shift, 1).reshape(r, c)


def _gelu_tanh(x):
    c = math.sqrt(2.0 / math.pi)
    return 0.5 * x * (1.0 + jnp.tanh(c * (x + 0.044715 * (x * x * x))))


def _largest_tile(n, candidates):
    for c in candidates:
        if n % c == 0:
            return c
    raise ValueError(f"no tile in {candidates} divides {n}")


def _cast_rows(src_ref, dst_ref, chunk):
    rows = src_ref.shape[0]
    chunk = min(chunk, rows)

    def body(k, c):
        r0 = pl.multiple_of(k * chunk, chunk)
        dst_ref[pl.ds(r0, chunk), :] = src_ref[pl.ds(r0, chunk), :].astype(dst_ref.dtype)
        return c

    lax.fori_loop(0, rows // chunk, body, 0)


def _adamod_kernel(cv_ref, w_ref, b_ref, o_ref):
    cv = cv_ref[...]
    s = cv * _sigmoid(cv)
    o_ref[...] = jnp.dot(s.astype(BF16), w_ref[...].astype(BF16),
                         preferred_element_type=F32) + b_ref[...]


def _ada_mod_all(cvecs, w_mod, b_mod):
    depth, d, n6 = w_mod.shape
    rows = cvecs.shape[0]
    tn = _largest_tile(n6, (512, 256, 128))
    return pl.pallas_call(
        _adamod_kernel,
        grid=(depth, n6 // tn),
        in_specs=[
            pl.BlockSpec((rows, d), lambda l, j: (0, 0)),
            pl.BlockSpec((None, d, tn), lambda l, j: (l, 0, j)),
            pl.BlockSpec((None, 1, tn), lambda l, j: (l, 0, j)),
        ],
        out_specs=pl.BlockSpec((None, rows, tn), lambda l, j: (l, 0, j)),
        out_shape=jax.ShapeDtypeStruct((depth, rows, n6), F32),
        compiler_params=_cparams(("arbitrary", "arbitrary")),
        name="ada_mod",
    )(cvecs, w_mod, b_mod.reshape(depth, 1, n6))


def _mod_row_map(n_prompt_tiles, tiles_per_latent_seq):
    def row(i):
        return jnp.where(i < n_prompt_tiles, 0, 1 + (i - n_prompt_tiles) // tiles_per_latent_seq)
    return row


def _norm_mod(x, g, mod_ref, shift_row, scale_row):
    ms = jnp.mean(x * x, axis=-1, keepdims=True)
    y = x * lax.rsqrt(ms + EPS) * g
    return y * (1.0 + mod_ref[scale_row:scale_row + 1, :]) + mod_ref[shift_row:shift_row + 1, :]


def _norm1_kernel(x_ref, g_ref, mod_ref, o_ref):
    o_ref[...] = _norm_mod(x_ref[...], g_ref[...], mod_ref, 0, 1).astype(o_ref.dtype)


def _norm1(x, g_all, layer, mod_l, row_map):
    n, d = x.shape
    tt = TOKEN_TILE
    return pl.pallas_call(
        _norm1_kernel,
        grid=(n // tt,),
        in_specs=[
            pl.BlockSpec((tt, d), lambda i: (i, 0)),
            pl.BlockSpec((None, 1, d), lambda i: (layer, 0, 0)),
            pl.BlockSpec((None, 6, d), lambda i: (row_map(i), 0, 0)),
        ],
        out_specs=pl.BlockSpec((tt, d), lambda i: (i, 0)),
        out_shape=jax.ShapeDtypeStruct((n, d), BF16),
        compiler_params=_cparams(("arbitrary",)),
        name="norm1_mod",
    )(x, g_all, mod_l)


def _norm2_router_kernel(x_ref, g_ref, mod_ref, wr_ref, br_ref, hn_ref, eid_ref, wts_ref, *,
                         n_groups, per_group):
    y = _norm_mod(x_ref[...], g_ref[...], mod_ref, 3, 4)
    hn_ref[...] = y
    logits = jnp.dot(y, wr_ref[...], preferred_element_type=F32,
                     precision=lax.Precision.HIGHEST) + br_ref[...]
    lane = lax.broadcasted_iota(jnp.int32, logits.shape, 1).astype(F32)
    big = float(LANES)
    gmask = lane < n_groups
    glog = jnp.where(gmask, logits, NEG_INF)
    gmax = jnp.max(glog, axis=-1, keepdims=True)
    gsum = jnp.sum(jnp.where(gmask, jnp.exp(glog - gmax), 0.0), axis=-1, keepdims=True)
    g_p = 1.0 / gsum
    g_i = jnp.min(jnp.where(gmask & (glog == gmax), lane, big), axis=-1, keepdims=True)
    lo = n_groups + g_i * per_group
    rmask = (lane >= lo) & (lane < lo + per_group)
    rlog = jnp.where(rmask, logits, NEG_INF)
    m1 = jnp.max(rlog, axis=-1, keepdims=True)
    i1 = jnp.min(jnp.where(rmask & (rlog == m1), lane, big), axis=-1, keepdims=True)
    rmask2 = rmask & (lane != i1)
    rlog2 = jnp.where(rmask2, logits, NEG_INF)
    m2 = jnp.max(rlog2, axis=-1, keepdims=True)
    i2 = jnp.min(jnp.where(rmask2 & (rlog2 == m2), lane, big), axis=-1, keepdims=True)
    e = jnp.exp(m2 - m1)
    w1 = g_p / (1.0 + e)
    w2 = w1 * e
    wts_ref[...] = jnp.where(lane == 0, w1, jnp.where(lane == 1, w2, 0.0))
    eid_ref[...] = jnp.where(lane == 0, i1 - n_groups,
                             jnp.where(lane == 1, i2 - n_groups, 0.0)).astype(jnp.int32)


def _norm2_router(x, g_all, layer, mod_l, row_map, w_rt, b_rt, n_groups, per_group):
    n, d = x.shape
    tt = TOKEN_TILE
    kern = functools.partial(_norm2_router_kernel, n_groups=n_groups, per_group=per_group)
    return pl.pallas_call(
        kern,
        grid=(n // tt,),
        in_specs=[
            pl.BlockSpec((tt, d), lambda i: (i, 0)),
            pl.BlockSpec((None, 1, d), lambda i: (layer, 0, 0)),
            pl.BlockSpec((None, 6, d), lambda i: (row_map(i), 0, 0)),
            pl.BlockSpec((d, LANES), lambda i: (0, 0)),
            pl.BlockSpec((1, LANES), lambda i: (0, 0)),
        ],
        out_specs=[
            pl.BlockSpec((tt, d), lambda i: (i, 0)),
            pl.BlockSpec((tt, LANES), lambda i: (i, 0)),
            pl.BlockSpec((tt, LANES), lambda i: (i, 0)),
        ],
        out_shape=[
            jax.ShapeDtypeStruct((n, d), F32),
            jax.ShapeDtypeStruct((n, LANES), jnp.int32),
            jax.ShapeDtypeStruct((n, LANES), F32),
        ],
        compiler_params=_cparams(("arbitrary",)),
        name="norm2_router",
    )(x, g_all, mod_l, w_rt, b_rt)


def _final_norm_kernel(x_ref, g_ref, o_ref):
    x = x_ref[...]
    ms = jnp.mean(x * x, axis=-1, keepdims=True)
    o_ref[...] = x * lax.rsqrt(ms + EPS) * g_ref[...]


def _final_norm(x, g):
    n, d = x.shape
    tt = TOKEN_TILE
    return pl.pallas_call(
        _final_norm_kernel,
        grid=(n // tt,),
        in_specs=[pl.BlockSpec((tt, d), lambda i: (i, 0)),
                  pl.BlockSpec((1, d), lambda i: (0, 0))],
        out_specs=pl.BlockSpec((tt, d), lambda i: (i, 0)),
        out_shape=jax.ShapeDtypeStruct((n, d), F32),
        compiler_params=_cparams(("arbitrary",)),
        name="final_norm",
    )(x, g.reshape(1, d))


def _mm_kernel(x_ref, w_ref, o_ref, wbf_ref):
    @pl.when(pl.program_id(1) == 0)
    def _():
        _cast_rows(w_ref, wbf_ref, 512)

    o_ref[...] = jnp.dot(x_ref[...], wbf_ref[...], preferred_element_type=F32).astype(o_ref.dtype)


def _in_proj(hn, w_in_all, layer):
    m, k = hn.shape
    n = w_in_all.shape[2]
    tn = _largest_tile(n, (1024, 512, 256))
    tm = _largest_tile(m, (512, 256))
    return pl.pallas_call(
        _mm_kernel,
        grid=(n // tn, m // tm),
        in_specs=[
            pl.BlockSpec((tm, k), lambda j, i: (i, 0)),
            pl.BlockSpec((None, k, tn), lambda j, i: (layer, 0, j), pipeline_mode=pl.Buffered(1)),
        ],
        out_specs=pl.BlockSpec((tm, tn), lambda j, i: (i, j)),
        out_shape=jax.ShapeDtypeStruct((m, n), F32),
        scratch_shapes=[pltpu.VMEM((k, tn), BF16)],
        compiler_params=_cparams(("arbitrary", "arbitrary")),
        name="in_proj",
    )(hn, w_in_all)


MERGE_W_CHUNK = 512


def _merge_kernel(oap_ref, obp_ref, ocp_ref, oas_ref, obs_ref, ocs_ref, ga_ref, gb_ref, gc_ref, w_hbm,
                  o_ref, wbf, stage, sem, *, layer, n_ctx_tiles):
    d = oap_ref.shape[1]
    tn = o_ref.shape[1]
    ck = stage.shape[1]
    n_ck = wbf.shape[0] // ck

    @pl.when(pl.program_id(1) == 0)
    def _():
        col0 = pl.multiple_of(pl.program_id(0) * tn, tn)

        def copy(c, slot):
            r0 = pl.multiple_of(c * ck, ck)
            return pltpu.make_async_copy(w_hbm.at[layer, pl.ds(r0, ck), pl.ds(col0, tn)],
                                         stage.at[slot], sem.at[slot])

        copy(0, 0).start()

        def body(c, carry):
            slot = c & 1

            @pl.when(c + 1 < n_ck)
            def _():
                copy(c + 1, 1 - slot).start()

            copy(c, slot).wait()
            r0 = pl.multiple_of(c * ck, ck)
            wbf[pl.ds(r0, ck), :] = stage[slot].astype(BF16)
            return carry

        lax.fori_loop(0, n_ck, body, 0)

    def branch(o_ref_, g_ref_, r):
        gate = 0.5 * jnp.tanh(0.5 * g_ref_[...]) + 0.5
        return gate * jnp.dot(o_ref_[...], wbf[r * d:(r + 1) * d, :], preferred_element_type=F32)

    def merge(oa_ref, ob_ref, oc_ref):
        acc = branch(oa_ref, ga_ref, 0) + branch(ob_ref, gb_ref, 1) + branch(oc_ref, gc_ref, 2)
        o_ref[...] = acc.astype(o_ref.dtype)

    @pl.when(pl.program_id(1) < n_ctx_tiles)
    def _():
        merge(oap_ref, obp_ref, ocp_ref)

    @pl.when(pl.program_id(1) >= n_ctx_tiles)
    def _():
        merge(oas_ref, obs_ref, ocs_ref)


def _merge(mix_ctx, mix_lat, proj, gate_off, w_branch_all, layer):
    n_p, d = mix_ctx[0].shape
    m = n_p + mix_lat[0].shape[0]
    n = w_branch_all.shape[2]
    tn = _largest_tile(math.gcd(n, gate_off), (512, 256))
    tm = TOKEN_TILE
    ck = MERGE_W_CHUNK
    assert (3 * d) % ck == 0 and d % ck == 0 and n_p % tm == 0
    gb = gate_off // tn
    nb = n // tn
    npt = n_p // tm
    cspec = pl.BlockSpec((tm, d), lambda j, i: (jnp.minimum(i, npt - 1), 0))
    lspec = pl.BlockSpec((tm, d), lambda j, i: (jnp.maximum(i - npt, 0), 0))
    gspec = lambda r: pl.BlockSpec((tm, tn), lambda j, i: (i, gb + r * nb + j))
    return pl.pallas_call(
        functools.partial(_merge_kernel, layer=layer, n_ctx_tiles=npt),
        grid=(nb, m // tm),
        in_specs=[cspec, cspec, cspec, lspec, lspec, lspec, gspec(0), gspec(1), gspec(2),
                  pl.BlockSpec(memory_space=pl.ANY)],
        out_specs=pl.BlockSpec((tm, tn), lambda j, i: (i, j)),
        out_shape=jax.ShapeDtypeStruct((m, n), BF16),
        scratch_shapes=[pltpu.VMEM((3 * d, tn), BF16), pltpu.VMEM((2, ck, tn), F32),
                        pltpu.SemaphoreType.DMA((2,))],
        compiler_params=_cparams(("arbitrary", "arbitrary")),
        name="branch_merge",
    )(*mix_ctx, *mix_lat, proj, proj, proj, w_branch_all)


def _out_proj_kernel(h_ref, w_ref, x_ref, mod_ref, o_ref, wbf_ref):
    @pl.when(pl.program_id(1) == 0)
    def _():
        _cast_rows(w_ref, wbf_ref, 512)

    mix = jnp.dot(h_ref[...], wbf_ref[...], preferred_element_type=F32)
    o_ref[...] = x_ref[...] + mod_ref[2:3, :] * mix


def _out_proj(merged, w_out_all, layer, x, mod_l, row_map):
    m, k = merged.shape
    n = w_out_all.shape[2]
    tn = _largest_tile(n, (1024, 512, 256))
    tm = TOKEN_TILE
    return pl.pallas_call(
        _out_proj_kernel,
        grid=(n // tn, m // tm),
        in_specs=[
            pl.BlockSpec((tm, k), lambda j, i: (i, 0)),
            pl.BlockSpec((None, k, tn), lambda j, i: (layer, 0, j), pipeline_mode=pl.Buffered(1)),
            pl.BlockSpec((tm, tn), lambda j, i: (i, j)),
            pl.BlockSpec((None, 6, tn), lambda j, i: (row_map(i), 0, j)),
        ],
        out_specs=pl.BlockSpec((tm, tn), lambda j, i: (i, j)),
        out_shape=jax.ShapeDtypeStruct((m, n), F32),
        scratch_shapes=[pltpu.VMEM((k, tn), BF16)],
        compiler_params=_cparams(("arbitrary", "arbitrary")),
        name="out_proj",
    )(merged, w_out_all, x, mod_l)


RG_CHUNK = 128


def _rglru_kernel(rx_ref, rg_ref, cw_ref, cb_ref, waf_ref, wxf_ref, wab_ref, wxb_ref,
                  ba_ref, bx_ref, lam_ref, h0_ref, o_ref, st_ref,
                  af, uf, ab, ub, wbf, *, seq):
    tc = RG_CHUNK
    n_chunks = seq // tc
    c = rx_ref.shape[1]
    wbf[0] = waf_ref[...].astype(BF16)
    wbf[1] = wxf_ref[...].astype(BF16)
    wbf[2] = wab_ref[...].astype(BF16)
    wbf[3] = wxb_ref[...].astype(BF16)
    cw = cw_ref[...]
    cb = cb_ref[...]

    def softplus(z):
        return jnp.maximum(z, 0.0) + jnp.log1p(jnp.exp(-jnp.abs(z)))

    sp_f = softplus(-lam_ref[0:1, :])
    sp_b = softplus(-lam_ref[1:2, :])
    row8 = lax.broadcasted_iota(jnp.int32, (tc, c), 0) & (SUBLANES - 1)

    def prep(k, carry):
        r0 = pl.multiple_of(k * tc, tc)
        cur = rx_ref[pl.ds(r0, tc), :]
        p0 = pl.multiple_of(jnp.maximum(r0 - SUBLANES, 0), SUBLANES)
        n0 = pl.multiple_of(jnp.minimum(r0 + tc, seq - SUBLANES), SUBLANES)
        prev8 = jnp.where(k > 0, rx_ref[pl.ds(p0, SUBLANES), :], 0.0)
        next8 = jnp.where(k < n_chunks - 1, rx_ref[pl.ds(n0, SUBLANES), :], 0.0)
        ext = jnp.concatenate([prev8, cur, next8], axis=0)
        ne = tc + 2 * SUBLANES
        xm2 = pltpu.roll(ext, 2, 0)[SUBLANES:SUBLANES + tc]
        xm1 = pltpu.roll(ext, 1, 0)[SUBLANES:SUBLANES + tc]
        xp1 = pltpu.roll(ext, ne - 1, 0)[SUBLANES:SUBLANES + tc]
        xc = cb + xm2 * cw[0:1] + xm1 * cw[1:2] + cur * cw[2:3] + xp1 * cw[3:4]
        xcb = xc.astype(BF16)

        def gates(wi, d, sp):
            r = _sigmoid(jnp.dot(xcb, wbf[wi], preferred_element_type=F32) + ba_ref[d:d + 1, :])
            i = _sigmoid(jnp.dot(xcb, wbf[wi + 1], preferred_element_type=F32) + bx_ref[d:d + 1, :])
            log_a = (-RG_C) * r * sp
            a = jnp.exp(log_a)
            u = jnp.sqrt(1.0 - a * a) * i * xc
            return a, u

        a, u = gates(0, 0, sp_f)
        for dd in (1, 2, 4):
            msk = row8 >= dd
            a_s = jnp.where(msk, _roll8(a, dd), 1.0)
            u_s = jnp.where(msk, _roll8(u, dd), 0.0)
            u = a * u_s + u
            a = a * a_s
        af[pl.ds(r0, tc), :] = a
        uf[pl.ds(r0, tc), :] = u

        a, u = gates(2, 1, sp_b)
        for dd in (1, 2, 4):
            msk = row8 < SUBLANES - dd
            a_s = jnp.where(msk, _roll8(a, SUBLANES - dd), 1.0)
            u_s = jnp.where(msk, _roll8(u, SUBLANES - dd), 0.0)
            u = a * u_s + u
            a = a * a_s
        ab[pl.ds(r0, tc), :] = a
        ub[pl.ds(r0, tc), :] = u
        return carry

    lax.fori_loop(0, n_chunks, prep, 0)

    n8 = seq // SUBLANES

    def carry_step(k, carry):
        hf, hb = carry
        i0 = pl.multiple_of(k * SUBLANES, SUBLANES)
        hf_t = af[pl.ds(i0, SUBLANES), :] * hf + uf[pl.ds(i0, SUBLANES), :]
        uf[pl.ds(i0, SUBLANES), :] = hf_t
        j0 = pl.multiple_of((n8 - 1 - k) * SUBLANES, SUBLANES)
        hb_t = ab[pl.ds(j0, SUBLANES), :] * hb + ub[pl.ds(j0, SUBLANES), :]
        ub[pl.ds(j0, SUBLANES), :] = hb_t
        return hf_t[SUBLANES - 1:SUBLANES, :], hb_t[0:1, :]

    hf, hb = lax.fori_loop(0, n8, carry_step, (h0_ref[0:1, :], h0_ref[1:2, :]))
    st_ref[0:1, :] = hf
    st_ref[1:2, :] = hb

    def finish(k, carry):
        r0 = pl.multiple_of(k * tc, tc)
        hsum = uf[pl.ds(r0, tc), :] + ub[pl.ds(r0, tc), :]
        o_ref[pl.ds(r0, tc), :] = (hsum * _gelu_tanh(rg_ref[pl.ds(r0, tc), :])).astype(o_ref.dtype)
        return carry

    lax.fori_loop(0, n_chunks, finish, 0)


def _rglru(proj, row_blk0, n_seq, seq, d, conv_w, conv_b, rg_wa, rg_wx, rg_ba, rg_bx, rg_lambda,
           h0, layer):
    nb, c = rg_wa.shape[2], rg_wa.shape[3]
    kern = functools.partial(_rglru_kernel, seq=seq)
    wspec = lambda direction: pl.BlockSpec((None, None, None, c, c),
                                           lambda b, n: (layer, direction, n, 0, 0))
    vec2 = pl.BlockSpec((None, 2, c), lambda b, n: (layer, 0, n))
    return pl.pallas_call(
        kern,
        grid=(n_seq, nb),
        in_specs=[
            pl.BlockSpec((seq, c), lambda b, n: (row_blk0 + b, n)),
            pl.BlockSpec((seq, c), lambda b, n: (row_blk0 + b, nb + n)),
            pl.BlockSpec((None, conv_w.shape[1], c), lambda b, n: (layer, 0, n)),
            pl.BlockSpec((None, 1, c), lambda b, n: (layer, 0, n)),
            wspec(0), wspec(0), wspec(1), wspec(1),
            vec2, vec2, vec2,
            pl.BlockSpec((None, 2, c), lambda b, n: (b, 0, n)),
        ],
        out_specs=[
            pl.BlockSpec((seq, c), lambda b, n: (b, n)),
            pl.BlockSpec((None, 2, c), lambda b, n: (b, 0, n)),
        ],
        out_shape=[
            jax.ShapeDtypeStruct((n_seq * seq, d), BF16),
            jax.ShapeDtypeStruct((n_seq, 2, d), F32),
        ],
        scratch_shapes=[pltpu.VMEM((seq, c), F32)] * 4 + [pltpu.VMEM((4, c, c), BF16)],
        compiler_params=_cparams(("arbitrary", "arbitrary")),
        name="rglru",
    )(proj, proj, conv_w, conv_b, rg_wa, rg_wx, rg_wa, rg_wx, rg_ba, rg_bx, rg_lambda, h0)


def _rope(x, cos_f, sin_f):
    return x * cos_f + pltpu.roll(x, HEAD_DIM // 2, 1) * sin_f


def _sink_softmax_pv(s, sink_col, v):
    m = jnp.maximum(jnp.max(s, axis=-1, keepdims=True), sink_col)
    p = jnp.exp(s - m)
    denom = jnp.sum(p, axis=-1, keepdims=True) + jnp.exp(sink_col - m)
    o = jnp.dot(p.astype(BF16), v, preferred_element_type=F32)
    return o / denom


def _sink_col(sink_ref, layer, hkv, group, rows_per_head):
    rows = group * rows_per_head
    assert rows_per_head & (rows_per_head - 1) == 0
    head = lax.broadcasted_iota(jnp.int32, (rows, 1), 0) >> (rows_per_head.bit_length() - 1)
    col = jnp.zeros((rows, 1), F32)
    for g in range(group):
        col = jnp.where(head == g, sink_ref[layer, hkv * group + g], col)
    return col


def _win_ctx_kernel(sink_ref, q_ref, k_ref, v_ref, o_ref, *, layer, group):
    hkv = pl.program_id(1)
    tq = q_ref.shape[0]
    q = jnp.concatenate([q_ref[:, g * HEAD_DIM:(g + 1) * HEAD_DIM] * ATTN_SCALE
                         for g in range(group)], axis=0)
    s = lax.dot_general(q.astype(BF16), k_ref[...].astype(BF16), _DN_QKT,
                        preferred_element_type=F32)
    o = _sink_softmax_pv(s, _sink_col(sink_ref, layer, hkv, group, tq), v_ref[...].astype(BF16))
    for g in range(group):
        o_ref[:, g * HEAD_DIM:(g + 1) * HEAD_DIM] = o[g * tq:(g + 1) * tq].astype(o_ref.dtype)


def _win_ctx(proj, n_seq, seq, q_off, k_off, v_off, n_kv, group, win_sink, layer):
    gw = group * HEAD_DIM
    kern = functools.partial(_win_ctx_kernel, layer=layer, group=group)
    return pl.pallas_call(
        kern,
        grid_spec=pltpu.PrefetchScalarGridSpec(
            num_scalar_prefetch=1,
            grid=(n_seq, n_kv),
            in_specs=[
                pl.BlockSpec((seq, gw), lambda b, h, s: (b, q_off // gw + h)),
                pl.BlockSpec((seq, HEAD_DIM), lambda b, h, s: (b, k_off // HEAD_DIM + h)),
                pl.BlockSpec((seq, HEAD_DIM), lambda b, h, s: (b, v_off // HEAD_DIM + h)),
            ],
            out_specs=pl.BlockSpec((seq, gw), lambda b, h, s: (b, h)),
        ),
        out_shape=jax.ShapeDtypeStruct((n_seq * seq, n_kv * gw), BF16),
        compiler_params=_cparams(("arbitrary", "arbitrary")),
        name="win_attn_ctx",
    )(win_sink, proj, proj, proj)


def _win_lat_kernel(sink_ref, q_ref, kp_ref, kc_ref, kn_ref, vp_ref, vc_ref, vn_ref, ck_ref, cv_ref,
                    cq_ref, sq_ref, cp_ref, sp_ref, cc_ref, sc_ref, cn_ref, sn_ref, o_ref, *,
                    layer, group, n_qblk):
    hkv = pl.program_id(1)
    qi = pl.program_id(2)
    tq = q_ref.shape[0]
    past = ck_ref.shape[0]
    cq, sq = cq_ref[...], sq_ref[...]
    q = jnp.concatenate([_rope(q_ref[:, g * HEAD_DIM:(g + 1) * HEAD_DIM], cq, sq) * ATTN_SCALE
                         for g in range(group)], axis=0).astype(BF16)
    k = jnp.concatenate([
        ck_ref[...].astype(BF16),
        _rope(kp_ref[...], cp_ref[...], sp_ref[...]).astype(BF16),
        _rope(kc_ref[...], cc_ref[...], sc_ref[...]).astype(BF16),
        _rope(kn_ref[...], cn_ref[...], sn_ref[...]).astype(BF16)], axis=0)
    v = jnp.concatenate([cv_ref[...].astype(BF16), vp_ref[...].astype(BF16),
                         vc_ref[...].astype(BF16), vn_ref[...].astype(BF16)], axis=0)
    s = lax.dot_general(q, k, _DN_QKT, preferred_element_type=F32)
    r = lax.broadcasted_iota(jnp.int32, s.shape, 0) & (tq - 1)
    c = lax.broadcasted_iota(jnp.int32, s.shape, 1) - past
    in_prev = (c >= 0) & (c < tq)
    in_next = c >= 2 * tq
    bad = (in_prev & ((c < r + (tq - WINDOW)) | (qi == 0))) | \
          (in_next & ((c - 2 * tq > r - (tq - WINDOW)) | (qi == n_qblk - 1)))
    s = jnp.where(bad, NEG_INF, s)
    o = _sink_softmax_pv(s, _sink_col(sink_ref, layer, hkv, group, tq), v)
    for g in range(group):
        o_ref[:, g * HEAD_DIM:(g + 1) * HEAD_DIM] = o[g * tq:(g + 1) * tq].astype(o_ref.dtype)


def _win_lat(proj, row0, n_seq, seq, q_off, k_off, v_off, n_kv, group, win_sink, layer,
             cache_k, cache_v, cos_f, sin_f):
    tq = WINDOW
    assert seq % tq == 0 and row0 % tq == 0
    nq = seq // tq
    gw = group * HEAD_DIM
    past = cache_k.shape[2]
    rb0 = row0 // tq
    kb, vb = k_off // HEAD_DIM, v_off // HEAD_DIM
    kern = functools.partial(_win_lat_kernel, layer=layer, group=group, n_qblk=nq)
    prv = lambda i: jnp.maximum(i - 1, 0)
    nxt = lambda i: jnp.minimum(i + 1, nq - 1)
    kv = lambda colb, f: pl.BlockSpec((tq, HEAD_DIM),
                                      lambda b, h, i, s: (rb0 + b * nq + f(i), colb + h))
    cache = pl.BlockSpec((None, None, past, HEAD_DIM), lambda b, h, i, s: (b, layer, 0, h))
    tab = lambda f: pl.BlockSpec((tq, HEAD_DIM), lambda b, h, i, s: (f(i), 0))
    same = lambda i: i
    return pl.pallas_call(
        kern,
        grid_spec=pltpu.PrefetchScalarGridSpec(
            num_scalar_prefetch=1,
            grid=(n_seq, n_kv, nq),
            in_specs=[
                pl.BlockSpec((tq, gw), lambda b, h, i, s: (rb0 + b * nq + i, q_off // gw + h)),
                kv(kb, prv), kv(kb, same), kv(kb, nxt),
                kv(vb, prv), kv(vb, same), kv(vb, nxt),
                cache, cache,
                tab(same), tab(same), tab(prv), tab(prv), tab(same), tab(same), tab(nxt), tab(nxt),
            ],
            out_specs=pl.BlockSpec((tq, gw), lambda b, h, i, s: (b * nq + i, h)),
        ),
        out_shape=jax.ShapeDtypeStruct((n_seq * seq, n_kv * gw), BF16),
        compiler_params=_cparams(("arbitrary", "arbitrary", "arbitrary")),
        name="win_attn_lat",
    )(win_sink, proj, proj, proj, proj, proj, proj, proj, cache_k, cache_v,
      cos_f, sin_f, cos_f, sin_f, cos_f, sin_f, cos_f, sin_f)


def _diff_lambda(l4, lam_init):
    t1 = jnp.sum(l4[0:1, :] * l4[1:2, :], axis=-1, keepdims=True)
    t2 = jnp.sum(l4[2:3, :] * l4[3:4, :], axis=-1, keepdims=True)
    return jnp.exp(t1) - jnp.exp(t2) + lam_init


ATTN_SCALE = HEAD_DIM ** -0.5
_DN_QKT = (((1,), (1,)), ((), ()))


def _subln(o, g, lam_init):
    ms = jnp.mean(o * o, axis=-1, keepdims=True)
    return o * lax.rsqrt(ms + EPS) * g * (1.0 - lam_init)


def _diff_core(q0, q1, k0, k1, v, lam, g, lam_init):
    s0 = lax.dot_general(q0, k0, _DN_QKT, preferred_element_type=F32)
    s1 = lax.dot_general(q1, k1, _DN_QKT, preferred_element_type=F32)
    e0 = jnp.exp(s0 - jnp.max(s0, axis=-1, keepdims=True))
    e1 = jnp.exp(s1 - jnp.max(s1, axis=-1, keepdims=True))
    c0 = 1.0 / jnp.sum(e0, axis=-1, keepdims=True)
    c1 = lam / jnp.sum(e1, axis=-1, keepdims=True)
    a = e0 * c0 - e1 * c1
    o = jnp.dot(a.astype(BF16), v, preferred_element_type=F32)
    return _subln(o, g, lam_init)


def _diff_ctx_kernel(q_ref, k_ref, v_ref, l4_ref, g_ref, o_ref, *, lam_init):
    lam = _diff_lambda(l4_ref[...], lam_init)
    hd = HEAD_DIM
    o = _diff_core((q_ref[:, 0:hd] * ATTN_SCALE).astype(BF16),
                   (q_ref[:, hd:2 * hd] * ATTN_SCALE).astype(BF16),
                   k_ref[:, 0:hd].astype(BF16), k_ref[:, hd:2 * hd].astype(BF16),
                   v_ref[...].astype(BF16), lam, g_ref[...], lam_init)
    o_ref[...] = o.astype(o_ref.dtype)


def _diff_ctx(proj, n_seq, seq, q_off, k_off, v_off, n_heads, diff_lambda, subln_g, layer, lam_init):
    hw = 2 * HEAD_DIM
    kern = functools.partial(_diff_ctx_kernel, lam_init=lam_init)
    blk = lambda off: pl.BlockSpec((seq, hw), lambda b, h: (b, off // hw + h))
    return pl.pallas_call(
        kern,
        grid=(n_seq, n_heads),
        in_specs=[blk(q_off), blk(k_off), blk(v_off),
                  pl.BlockSpec((None, 4, HEAD_DIM), lambda b, h: (layer, 0, 0)),
                  pl.BlockSpec((None, 1, hw), lambda b, h: (layer, 0, 0))],
        out_specs=pl.BlockSpec((seq, hw), lambda b, h: (b, h)),
        out_shape=jax.ShapeDtypeStruct((n_seq * seq, n_heads * hw), BF16),
        compiler_params=_cparams(("arbitrary", "arbitrary")),
        name="diff_attn_ctx",
    )(proj, proj, proj, diff_lambda, subln_g)


DIFF_KV_CHUNK = 256


def _diff_lat_kernel(q_ref, k_ref, v_ref, ck_ref, cv_ref, cq_ref, sq_ref, call_ref, sall_ref,
                     l4_ref, g_ref, o_ref, kall, vall, *, lam_init):
    hd = HEAD_DIM
    past = ck_ref.shape[0]
    seq = k_ref.shape[0]

    @pl.when(pl.program_id(2) == 0)
    def _():
        kall[0:past, :] = ck_ref[...].astype(BF16)
        vall[0:past, :] = cv_ref[...].astype(BF16)

        def fill(i, carry):
            r0 = pl.multiple_of(i * DIFF_KV_CHUNK, DIFF_KV_CHUNK)
            rows = pl.ds(r0, DIFF_KV_CHUNK)
            dst = pl.ds(past + r0, DIFF_KV_CHUNK)
            cs, sn = call_ref[rows, :], sall_ref[rows, :]
            kall[dst, 0:hd] = _rope(k_ref[rows, 0:hd], cs, sn).astype(BF16)
            kall[dst, hd:2 * hd] = _rope(k_ref[rows, hd:2 * hd], cs, sn).astype(BF16)
            vall[dst, :] = v_ref[rows, :].astype(BF16)
            return carry

        lax.fori_loop(0, seq // DIFF_KV_CHUNK, fill, 0)

    lam = _diff_lambda(l4_ref[...], lam_init)
    cq, sq = cq_ref[...], sq_ref[...]
    q0 = (_rope(q_ref[:, 0:hd], cq, sq) * ATTN_SCALE).astype(BF16)
    q1 = (_rope(q_ref[:, hd:2 * hd], cq, sq) * ATTN_SCALE).astype(BF16)
    o = _diff_core(q0, q1, kall[:, 0:hd], kall[:, hd:2 * hd], vall[...], lam, g_ref[...], lam_init)
    o_ref[...] = o.astype(o_ref.dtype)


def _diff_lat(proj, row0, n_seq, seq, q_off, k_off, v_off, n_heads, diff_lambda, subln_g, layer,
              lam_init, cache_k, cache_v, cos_f, sin_f):
    hw = 2 * HEAD_DIM
    tq = 256
    kc = DIFF_KV_CHUNK
    assert seq % kc == 0 and row0 % seq == 0 and seq % tq == 0
    nq = seq // tq
    past = cache_k.shape[2]
    sb0 = row0 // seq
    kern = functools.partial(_diff_lat_kernel, lam_init=lam_init)
    full = lambda off: pl.BlockSpec((seq, hw), lambda b, h, i: (sb0 + b, off // hw + h))
    cache = pl.BlockSpec((None, None, past, hw), lambda b, h, i: (b, layer, 0, h))
    return pl.pallas_call(
        kern,
        grid=(n_seq, n_heads, nq),
        in_specs=[
            pl.BlockSpec((tq, hw), lambda b, h, i: (sb0 * nq + b * nq + i, q_off // hw + h)),
            full(k_off), full(v_off), cache, cache,
            pl.BlockSpec((tq, HEAD_DIM), lambda b, h, i: (i, 0)),
            pl.BlockSpec((tq, HEAD_DIM), lambda b, h, i: (i, 0)),
            pl.BlockSpec((seq, HEAD_DIM), lambda b, h, i: (0, 0)),
            pl.BlockSpec((seq, HEAD_DIM), lambda b, h, i: (0, 0)),
            pl.BlockSpec((None, 4, HEAD_DIM), lambda b, h, i: (layer, 0, 0)),
            pl.BlockSpec((None, 1, hw), lambda b, h, i: (layer, 0, 0)),
        ],
        out_specs=pl.BlockSpec((tq, hw), lambda b, h, i: (b * nq + i, h)),
        out_shape=jax.ShapeDtypeStruct((n_seq * seq, n_heads * hw), BF16),
        scratch_shapes=[pltpu.VMEM((past + seq, hw), BF16), pltpu.VMEM((past + seq, hw), BF16)],
        compiler_params=_cparams(("arbitrary", "arbitrary", "arbitrary")),
        name="diff_attn_lat",
    )(proj, proj, proj, cache_k, cache_v, cos_f, sin_f, cos_f, sin_f, diff_lambda, subln_g)


def _moe_plan(eid, n_experts, rows_per_blk):
    n = eid.shape[0]
    a = n * TOP_K
    flat_e = eid.reshape(-1)
    onehot = (flat_e[:, None] == jnp.arange(n_experts, dtype=jnp.int32)[None, :]).astype(jnp.int32)
    csum = jnp.cumsum(onehot, axis=0)
    rank = jnp.take_along_axis(csum, flat_e[:, None], axis=1)[:, 0] - 1
    counts = csum[-1]
    nblk_e = (counts + rows_per_blk - 1) // rows_per_blk
    blk_end = jnp.cumsum(nblk_e)
    blk_start = blk_end - nblk_e
    dest = (blk_start[flat_e] * rows_per_blk + rank).astype(jnp.int32)
    n_blk = a // rows_per_blk + n_experts
    n_used = blk_end[-1].astype(jnp.int32)
    slot_tok = jnp.zeros((n_blk * rows_per_blk,), jnp.int32).at[dest].set(
        jnp.arange(a, dtype=jnp.int32) // TOP_K)
    blk_ids = jnp.arange(n_blk, dtype=jnp.int32)
    blk_e = jnp.minimum(jnp.searchsorted(blk_end, blk_ids, side='right'), n_experts - 1)
    blk_e = jnp.where(blk_ids < n_used, blk_e, blk_e[jnp.maximum(n_used - 1, 0)]).astype(jnp.int32)
    first = jnp.concatenate([jnp.ones((1,), jnp.int32),
                             (blk_e[1:] != blk_e[:-1]).astype(jnp.int32)])
    return slot_tok, dest, blk_e, first, n_used.reshape(1)


def _gather_rows_kernel(tok_ref, x_hbm, o_ref, buf, sem, *, rows_per_step):
    i = pl.program_id(0)
    slot = i & 1

    def issue(step, dst_slot):
        base = step * rows_per_step

        def body(r, carry):
            tok = tok_ref[base + r]
            pltpu.make_async_copy(x_hbm.at[pl.ds(tok, 1)], buf.at[dst_slot, pl.ds(r, 1)],
                                  sem.at[dst_slot]).start()
            return carry

        lax.fori_loop(0, rows_per_step, body, 0)

    @pl.when(i == 0)
    def _():
        issue(0, 0)

    @pl.when(i + 1 < pl.num_programs(0))
    def _():
        issue(i + 1, 1 - slot)

    pltpu.make_async_copy(x_hbm.at[pl.ds(0, rows_per_step)], buf.at[slot], sem.at[slot]).wait()
    o_ref[...] = buf[slot].astype(o_ref.dtype)


def _gather_rows(x, slot_tok):
    rows = slot_tok.shape[0]
    d = x.shape[1]
    rps = MOE_ROWS
    kern = functools.partial(_gather_rows_kernel, rows_per_step=rps)
    return pl.pallas_call(
        kern,
        grid_spec=pltpu.PrefetchScalarGridSpec(
            num_scalar_prefetch=1,
            grid=(rows // rps,),
            in_specs=[pl.BlockSpec(memory_space=pl.ANY)],
            out_specs=pl.BlockSpec((rps, d), lambda i, tok: (i, 0)),
            scratch_shapes=[pltpu.VMEM((2, rps, d), x.dtype), pltpu.SemaphoreType.DMA((2,))],
        ),
        out_shape=jax.ShapeDtypeStruct((rows, d), BF16),
        compiler_params=_cparams(("arbitrary",)),
        name="moe_gather",
    )(slot_tok, x)


def _gmm1_kernel(be_ref, first_ref, nused_ref, x_ref, wg_ref, wu_ref, o_ref, wgb, wub):
    blk = pl.program_id(1)

    @pl.when(first_ref[blk] == 1)
    def _():
        _cast_rows(wg_ref, wgb, 512)
        _cast_rows(wu_ref, wub, 512)

    @pl.when(blk < nused_ref[0])
    def _():
        x = x_ref[...]
        g = jnp.dot(x, wgb[...], preferred_element_type=F32)
        u = jnp.dot(x, wub[...], preferred_element_type=F32)
        o_ref[...] = (g * _sigmoid(g) * u).astype(o_ref.dtype)

    @pl.when(blk >= nused_ref[0])
    def _():
        o_ref[...] = jnp.zeros(o_ref.shape, o_ref.dtype)


def _gmm1(xs, w_gate_all, w_up_all, layer, blk_e, first, n_used):
    rows, d = xs.shape
    de = w_gate_all.shape[3]
    tn = _largest_tile(de, (512, 256, 128))
    r = MOE_ROWS
    wspec = pl.BlockSpec((None, None, d, tn), lambda j, b, be, fi, nu: (layer, be[b], 0, j))
    return pl.pallas_call(
        _gmm1_kernel,
        grid_spec=pltpu.PrefetchScalarGridSpec(
            num_scalar_prefetch=3,
            grid=(de // tn, rows // r),
            in_specs=[pl.BlockSpec((r, d), lambda j, b, be, fi, nu: (b, 0)), wspec, wspec],
            out_specs=pl.BlockSpec((r, tn), lambda j, b, be, fi, nu: (b, j)),
            scratch_shapes=[pltpu.VMEM((d, tn), BF16), pltpu.VMEM((d, tn), BF16)],
        ),
        out_shape=jax.ShapeDtypeStruct((rows, de), BF16),
        compiler_params=_cparams(("arbitrary", "arbitrary")),
        name="moe_gate_up",
    )(blk_e, first, n_used, xs, w_gate_all, w_up_all)


def _gmm2_kernel(be_ref, first_ref, nused_ref, h_ref, wd_ref, o_ref, wdb):
    blk = pl.program_id(1)

    @pl.when(first_ref[blk] == 1)
    def _():
        _cast_rows(wd_ref, wdb, 512)

    @pl.when(blk < nused_ref[0])
    def _():
        o_ref[...] = jnp.dot(h_ref[...], wdb[...], preferred_element_type=F32)

    @pl.when(blk >= nused_ref[0])
    def _():
        o_ref[...] = jnp.zeros(o_ref.shape, o_ref.dtype)


def _gmm2(h, w_down_all, layer, blk_e, first, n_used):
    rows, de = h.shape
    d = w_down_all.shape[3]
    tn = _largest_tile(d, (2048, 1024, 512, 256))
    r = MOE_ROWS
    return pl.pallas_call(
        _gmm2_kernel,
        grid_spec=pltpu.PrefetchScalarGridSpec(
            num_scalar_prefetch=3,
            grid=(d // tn, rows // r),
            in_specs=[
                pl.BlockSpec((r, de), lambda j, b, be, fi, nu: (b, 0)),
                pl.BlockSpec((None, None, de, tn), lambda j, b, be, fi, nu: (layer, be[b], 0, j)),
            ],
            out_specs=pl.BlockSpec((r, tn), lambda j, b, be, fi, nu: (b, j)),
            scratch_shapes=[pltpu.VMEM((de, tn), BF16)],
        ),
        out_shape=jax.ShapeDtypeStruct((rows, d), F32),
        compiler_params=_cparams(("arbitrary", "arbitrary")),
        name="moe_down",
    )(blk_e, first, n_used, h, w_down_all)


def _combine_kernel(slot_ref, x_ref, wts_ref, mod_ref, ys_hbm, o_ref, buf, sem):
    tt = x_ref.shape[0]
    base = pl.program_id(0) * tt

    def issue(r, carry):
        for k in range(TOP_K):
            s = slot_ref[TOP_K * (base + r) + k]
            pltpu.make_async_copy(ys_hbm.at[pl.ds(s, 1)], buf.at[k, pl.ds(r, 1)], sem).start()
        return carry

    lax.fori_loop(0, tt, issue, 0)
    for k in range(TOP_K):
        pltpu.make_async_copy(ys_hbm.at[pl.ds(0, tt)], buf.at[k], sem).wait()
    w = wts_ref[...]
    y = w[:, 0:1] * buf[0] + w[:, 1:2] * buf[1]
    o_ref[...] = x_ref[...] + mod_ref[5:6, :] * y


def _combine(x, wts, mod_l, row_map, ys, slots):
    n, d = x.shape
    tt = TOKEN_TILE
    return pl.pallas_call(
        _combine_kernel,
        grid_spec=pltpu.PrefetchScalarGridSpec(
            num_scalar_prefetch=1,
            grid=(n // tt,),
            in_specs=[
                pl.BlockSpec((tt, d), lambda i, s: (i, 0)),
                pl.BlockSpec((tt, LANES), lambda i, s: (i, 0)),
                pl.BlockSpec((None, 6, d), lambda i, s: (row_map(i), 0, 0)),
                pl.BlockSpec(memory_space=pl.ANY),
            ],
            out_specs=pl.BlockSpec((tt, d), lambda i, s: (i, 0)),
            scratch_shapes=[pltpu.VMEM((TOP_K, tt, d), F32), pltpu.SemaphoreType.DMA(())],
        ),
        out_shape=jax.ShapeDtypeStruct((n, d), F32),
        compiler_params=_cparams(("arbitrary",)),
        name="moe_combine",
    )(slots, x, wts, mod_l, ys)


def _rope_tables(seq):
    rows = seq // GRID_W
    row = jnp.repeat(jnp.arange(rows), GRID_W).astype(F32)
    col = jnp.tile(jnp.arange(GRID_W), rows).astype(F32)
    n_freq = HEAD_DIM // 4
    inv = 1.0 / (ROPE_BASE ** (jnp.arange(n_freq, dtype=F32) / n_freq))
    ang = jnp.concatenate([row[:, None] * inv, col[:, None] * inv], axis=-1)
    cos, sin = jnp.cos(ang), jnp.sin(ang)
    return jnp.concatenate([cos, cos], axis=-1), jnp.concatenate([-sin, sin], axis=-1)


def kernel(x_prompt, x_sample, c, cache_win_k, cache_win_v, cache_diff_k, cache_diff_v, state_rnn, c_ctx, w_mod, b_mod, norm1_g, norm2_g, final_g, w_in, conv_w, conv_b, rg_wa, rg_ba, rg_wx, rg_bx, rg_lambda, win_sink, diff_lambda, diff_subln_g, w_branch, w_out, moe_w_group, moe_b_group, moe_w_router, moe_b_router, moe_w_gate, moe_w_up, moe_w_down):
    bp, tp, d = x_prompt.shape
    bs, ts, _ = x_sample.shape
    depth = w_in.shape[0]
    past = cache_win_k.shape[2]
    n_kv = cache_win_k.shape[3]
    n_win = win_sink.shape[1]
    group = n_win // n_kv
    n_diff = cache_diff_k.shape[3]
    d_rnn = rg_ba.shape[2]
    n_groups = moe_w_group.shape[2]
    n_experts = moe_w_router.shape[2]
    per_group = n_experts // n_groups
    n_p, n_s = bp * tp, bs * ts
    tt = TOKEN_TILE
    assert tp % tt == 0 and ts % tt == 0 and n_p % ts == 0 and d_rnn == d
    assert n_groups + n_experts <= LANES

    sections = (d_rnn, d_rnn, n_win * HEAD_DIM, n_kv * HEAD_DIM, n_kv * HEAD_DIM,
                n_diff * 2 * HEAD_DIM, n_diff * 2 * HEAD_DIM, n_diff * 2 * HEAD_DIM, 3 * d)
    offs = [0]
    for s_ in sections:
        offs.append(offs[-1] + s_)
    _, _, o_wq, o_wk, o_wv, o_dq, o_dk, o_dv, o_mg = offs[:9]

    row_map = _mod_row_map(n_p // tt, ts // tt)

    n_rows = -(-(1 + bs) // SUBLANES) * SUBLANES
    cvecs = jnp.zeros((n_rows, d), F32).at[0].set(c_ctx).at[1:1 + bs].set(c)
    mod = _ada_mod_all(cvecs, w_mod, b_mod).reshape(depth, n_rows, 6, d)

    cos_f, sin_f = _rope_tables(ts)
    ck_win = cache_win_k.reshape(bs, depth, past, n_kv * HEAD_DIM)
    cv_win = cache_win_v.reshape(bs, depth, past, n_kv * HEAD_DIM)
    ck_diff = cache_diff_k.reshape(bs, depth, past, n_diff * 2 * HEAD_DIM)
    cv_diff = cache_diff_v.reshape(bs, depth, past, n_diff * 2 * HEAD_DIM)
    w_rt = jnp.zeros((depth, d, LANES), F32).at[:, :, :n_groups].set(moe_w_group) \
        .at[:, :, n_groups:n_groups + n_experts].set(moe_w_router)
    b_rt = jnp.zeros((depth, 1, LANES), F32).at[:, 0, :n_groups].set(moe_b_group) \
        .at[:, 0, n_groups:n_groups + n_experts].set(moe_b_router)
    h0_prompt = jnp.zeros((bp, 2, d_rnn), F32)
    norm1_g = norm1_g.reshape(depth, 1, d)
    norm2_g = norm2_g.reshape(depth, 1, d)
    conv_b = conv_b.reshape(depth, 1, d_rnn)
    diff_subln_g = diff_subln_g.reshape(depth, 1, 2 * HEAD_DIM)

    x = jnp.concatenate([x_prompt.reshape(n_p, d), x_sample.reshape(n_s, d)], axis=0)
    new_wk, new_wv, new_dk, new_dv, new_st = [], [], [], [], []
    for l in range(depth):
        lam_init = 0.8 - 0.6 * math.exp(-0.3 * l)
        mod_l = mod[l]
        hn = _norm1(x, norm1_g, l, mod_l, row_map)
        proj = _in_proj(hn, w_in, l)

        rg_args = (conv_w, conv_b, rg_wa, rg_wx, rg_ba, rg_bx, rg_lambda)
        oa_p, st_p = _rglru(proj, 0, bp, tp, d_rnn, *rg_args, h0_prompt, l)
        oa_s, _ = _rglru(proj, n_p // ts, bs, ts, d_rnn, *rg_args, state_rnn[:, l], l)

        ob_p = _win_ctx(proj, bp, tp, o_wq, o_wk, o_wv, n_kv, group, win_sink, l)
        ob_s = _win_lat(proj, n_p, bs, ts, o_wq, o_wk, o_wv, n_kv, group, win_sink, l,
                        ck_win, cv_win, cos_f, sin_f)
        oc_p = _diff_ctx(proj, bp, tp, o_dq, o_dk, o_dv, n_diff, diff_lambda, diff_subln_g, l, lam_init)
        oc_s = _diff_lat(proj, n_p, bs, ts, o_dq, o_dk, o_dv, n_diff, diff_lambda, diff_subln_g, l,
                         lam_init, ck_diff, cv_diff, cos_f, sin_f)

        merged = _merge((oa_p, ob_p, oc_p), (oa_s, ob_s, oc_s), proj, o_mg, w_branch, l)
        x = _out_proj(merged, w_out, l, x, mod_l, row_map)

        hn2, eid, wts = _norm2_router(x, norm2_g, l, mod_l, row_map, w_rt[l], b_rt[l],
                                      n_groups, per_group)
        slot_tok, dest, blk_e, first, n_used = _moe_plan(eid[:, :TOP_K], n_experts, MOE_ROWS)
        xs = _gather_rows(hn2, slot_tok)
        hmid = _gmm1(xs, moe_w_gate, moe_w_up, l, blk_e, first, n_used)
        ys = _gmm2(hmid, moe_w_down, l, blk_e, first, n_used)
        x = _combine(x, wts, mod_l, row_map, ys, dest)

        pr = proj[:n_p]
        new_wk.append(pr[:, o_wk:o_wv].reshape(bp, tp, n_kv, HEAD_DIM))
        new_wv.append(pr[:, o_wv:o_dq].reshape(bp, tp, n_kv, HEAD_DIM))
        new_dk.append(pr[:, o_dk:o_dv].reshape(bp, tp, n_diff, 2, HEAD_DIM))
        new_dv.append(pr[:, o_dv:o_mg].reshape(bp, tp, n_diff, 2 * HEAD_DIM))
        new_st.append(st_p)

    y = _final_norm(x, final_g)
    return (y[:n_p].reshape(bp, tp, d), y[n_p:].reshape(bs, ts, d),
            jnp.stack(new_wk, axis=1), jnp.stack(new_wv, axis=1),
            jnp.stack(new_dk, axis=1), jnp.stack(new_dv, axis=1), jnp.stack(new_st, axis=1))
```

```python
import functools
import math

import jax
import jax.numpy as jnp
from jax import lax
from jax.experimental import pallas as pl
from jax.experimental.pallas import tpu as pltpu

F32 = jnp.float32
BF16 = jnp.bfloat16

HEAD_DIM = 128
GRID_W = 64
ROPE_BASE = 10000.0
WINDOW = 128
CONV_LEFT = 2
RG_C = 8.0
TOP_K = 2
EPS = 1e-6
NEG_INF = -1e30

V7X_VMEM_LIMIT_BYTES = 56 * 1024 * 1024
LANES = 128
SUBLANES = 8

TOKEN_TILE = 256
MOE_ROWS = 256
DMA_ISSUE_UNROLL = 8


def _cparams(sem):
    return pltpu.CompilerParams(dimension_semantics=sem, vmem_limit_bytes=V7X_VMEM_LIMIT_BYTES)


def _sigmoid(x):
    return 0.5 * jnp.tanh(0.5 * x) + 0.5


def _roll8(x, shift):
    r, c = x.shape
    return pltpu.roll(x.reshape(r // SUBLANES, SUBLANES, c), shift, 1).reshape(r, c)


def _pack_bf16_pairs(x):
    w = x.shape[1] // 2
    lo = lax.bitcast_convert_type(x[:, :w].astype(BF16).astype(F32), jnp.uint32) >> 16
    hi = lax.bitcast_convert_type(x[:, w:].astype(BF16).astype(F32), jnp.uint32) & jnp.uint32(0xFFFF0000)
    return hi | lo


def _unpack_bf16_pairs(p):
    lo = lax.bitcast_convert_type(p << 16, F32)
    hi = lax.bitcast_convert_type(p & jnp.uint32(0xFFFF0000), F32)
    return lo, hi


def _gelu_tanh(x):
    c = math.sqrt(2.0 / math.pi)
    return 0.5 * x * (1.0 + jnp.tanh(c * (x + 0.044715 * (x * x * x))))


def _largest_tile(n, candidates):
    for c in candidates:
        if n % c == 0:
            return c
    raise ValueError(f"no tile in {candidates} divides {n}")


def _cast_rows(src_ref, dst_ref, chunk):
    rows = src_ref.shape[0]
    chunk = min(chunk, rows)

    def body(k, c):
        r0 = pl.multiple_of(k * chunk, chunk)
        dst_ref[pl.ds(r0, chunk), :] = src_ref[pl.ds(r0, chunk), :].astype(dst_ref.dtype)
        return c

    lax.fori_loop(0, rows // chunk, body, 0)


def _adamod_kernel(cv_ref, w_ref, b_ref, o_ref):
    cv = cv_ref[...]
    s = cv * _sigmoid(cv)
    o_ref[...] = jnp.dot(s.astype(BF16), w_ref[...].astype(BF16),
                         preferred_element_type=F32) + b_ref[...]


def _ada_mod_all(cvecs, w_mod, b_mod):
    depth, d, n6 = w_mod.shape
    rows = cvecs.shape[0]
    tn = _largest_tile(n6, (512, 256, 128))
    return pl.pallas_call(
        _adamod_kernel,
        grid=(depth, n6 // tn),
        in_specs=[
            pl.BlockSpec((rows, d), lambda l, j: (0, 0)),
            pl.BlockSpec((None, d, tn), lambda l, j: (l, 0, j)),
            pl.BlockSpec((None, 1, tn), lambda l, j: (l, 0, j)),
        ],
        out_specs=pl.BlockSpec((None, rows, tn), lambda l, j: (l, 0, j)),
        out_shape=jax.ShapeDtypeStruct((depth, rows, n6), F32),
        compiler_params=_cparams(("arbitrary", "arbitrary")),
        name="ada_mod",
    )(cvecs, w_mod, b_mod.reshape(depth, 1, n6))


def _mod_row_map(n_prompt_tiles, tiles_per_latent_seq):
    def row(i):
        return jnp.where(i < n_prompt_tiles, 0, 1 + (i - n_prompt_tiles) // tiles_per_latent_seq)
    return row


def _norm_mod(x, g, mod_ref, shift_row, scale_row):
    ms = jnp.mean(x * x, axis=-1, keepdims=True)
    y = x * lax.rsqrt(ms + EPS) * g
    return y * (1.0 + mod_ref[scale_row:scale_row + 1, :]) + mod_ref[shift_row:shift_row + 1, :]


def _norm1_kernel(x_ref, g_ref, mod_ref, o_ref):
    o_ref[...] = _norm_mod(x_ref[...], g_ref[...], mod_ref, 0, 1).astype(o_ref.dtype)


def _norm1(x, g_all, layer, mod_l, row_map):
    n, d = x.shape
    tt = TOKEN_TILE
    return pl.pallas_call(
        _norm1_kernel,
        grid=(n // tt,),
        in_specs=[
            pl.BlockSpec((tt, d), lambda i: (i, 0)),
            pl.BlockSpec((None, 1, d), lambda i: (layer, 0, 0)),
            pl.BlockSpec((None, 6, d), lambda i: (row_map(i), 0, 0)),
        ],
        out_specs=pl.BlockSpec((tt, d), lambda i: (i, 0)),
        out_shape=jax.ShapeDtypeStruct((n, d), BF16),
        compiler_params=_cparams(("arbitrary",)),
        name="norm1_mod",
    )(x, g_all, mod_l)


def _norm2_router_kernel(x_ref, g_ref, mod_ref, wh_ref, wl_ref, br_ref, hn_ref, eid_ref, wts_ref, *,
                         n_groups, per_group):
    y = _norm_mod(x_ref[...], g_ref[...], mod_ref, 3, 4)
    hn_ref[...] = _pack_bf16_pairs(y)
    y_hi = y.astype(BF16)
    y_lo = (y - y_hi.astype(F32)).astype(BF16)
    w_hi = wh_ref[...]
    logits = (jnp.dot(y_hi, w_hi, preferred_element_type=F32)
              + (jnp.dot(y_lo, w_hi, preferred_element_type=F32)
                 + jnp.dot(y_hi, wl_ref[...], preferred_element_type=F32))) + br_ref[...]
    lane = lax.broadcasted_iota(jnp.int32, logits.shape, 1).astype(F32)
    big = float(LANES)
    gmask = lane < n_groups
    glog = jnp.where(gmask, logits, NEG_INF)
    gmax = jnp.max(glog, axis=-1, keepdims=True)
    gsum = jnp.sum(jnp.where(gmask, jnp.exp(glog - gmax), 0.0), axis=-1, keepdims=True)
    g_p = 1.0 / gsum
    g_i = jnp.min(jnp.where(gmask & (glog == gmax), lane, big), axis=-1, keepdims=True)
    lo = n_groups + g_i * per_group
    rmask = (lane >= lo) & (lane < lo + per_group)
    rlog = jnp.where(rmask, logits, NEG_INF)
    m1 = jnp.max(rlog, axis=-1, keepdims=True)
    i1 = jnp.min(jnp.where(rmask & (rlog == m1), lane, big), axis=-1, keepdims=True)
    rmask2 = rmask & (lane != i1)
    rlog2 = jnp.where(rmask2, logits, NEG_INF)
    m2 = jnp.max(rlog2, axis=-1, keepdims=True)
    i2 = jnp.min(jnp.where(rmask2 & (rlog2 == m2), lane, big), axis=-1, keepdims=True)
    e = jnp.exp(m2 - m1)
    w1 = g_p / (1.0 + e)
    w2 = w1 * e
    wts_ref[...] = jnp.where(lane == 0, w1, jnp.where(lane == 1, w2, 0.0))
    eid_ref[...] = jnp.where(lane == 0, i1 - n_groups,
                             jnp.where(lane == 1, i2 - n_groups, 0.0)).astype(jnp.int32)


def _norm2_router(x, g_all, layer, mod_l, row_map, w_rt_hi, w_rt_lo, b_rt, n_groups, per_group):
    n, d = x.shape
    tt = TOKEN_TILE
    kern = functools.partial(_norm2_router_kernel, n_groups=n_groups, per_group=per_group)
    return pl.pallas_call(
        kern,
        grid=(n // tt,),
        in_specs=[
            pl.BlockSpec((tt, d), lambda i: (i, 0)),
            pl.BlockSpec((None, 1, d), lambda i: (layer, 0, 0)),
            pl.BlockSpec((None, 6, d), lambda i: (row_map(i), 0, 0)),
            pl.BlockSpec((d, LANES), lambda i: (0, 0)),
            pl.BlockSpec((d, LANES), lambda i: (0, 0)),
            pl.BlockSpec((1, LANES), lambda i: (0, 0)),
        ],
        out_specs=[
            pl.BlockSpec((tt, d // 2), lambda i: (i, 0)),
            pl.BlockSpec((tt, LANES), lambda i: (i, 0)),
            pl.BlockSpec((tt, LANES), lambda i: (i, 0)),
        ],
        out_shape=[
            jax.ShapeDtypeStruct((n, d // 2), jnp.uint32),
            jax.ShapeDtypeStruct((n, LANES), jnp.int32),
            jax.ShapeDtypeStruct((n, LANES), F32),
        ],
        compiler_params=_cparams(("arbitrary",)),
        name="norm2_router",
    )(x, g_all, mod_l, w_rt_hi, w_rt_lo, b_rt)


def _final_norm_kernel(x_ref, g_ref, o_ref):
    x = x_ref[...]
    ms = jnp.mean(x * x, axis=-1, keepdims=True)
    o_ref[...] = x * lax.rsqrt(ms + EPS) * g_ref[...]


def _final_norm(x, g):
    n, d = x.shape
    tt = TOKEN_TILE
    return pl.pallas_call(
        _final_norm_kernel,
        grid=(n // tt,),
        in_specs=[pl.BlockSpec((tt, d), lambda i: (i, 0)),
                  pl.BlockSpec((1, d), lambda i: (0, 0))],
        out_specs=pl.BlockSpec((tt, d), lambda i: (i, 0)),
        out_shape=jax.ShapeDtypeStruct((n, d), F32),
        compiler_params=_cparams(("arbitrary",)),
        name="final_norm",
    )(x, g.reshape(1, d))


def _mm_kernel(x_ref, w_ref, o_ref, wbf_ref):
    @pl.when(pl.program_id(1) == 0)
    def _():
        _cast_rows(w_ref, wbf_ref, 512)

    o_ref[...] = jnp.dot(x_ref[...], wbf_ref[...], preferred_element_type=F32).astype(o_ref.dtype)


def _in_proj(hn, w_in_all, layer):
    m, k = hn.shape
    n = w_in_all.shape[2]
    tn = _largest_tile(n, (1024, 512, 256))
    tm = _largest_tile(m, (512, 256))
    return pl.pallas_call(
        _mm_kernel,
        grid=(n // tn, m // tm),
        in_specs=[
            pl.BlockSpec((tm, k), lambda j, i: (i, 0)),
            pl.BlockSpec((None, k, tn), lambda j, i: (layer, 0, j), pipeline_mode=pl.Buffered(1)),
        ],
        out_specs=pl.BlockSpec((tm, tn), lambda j, i: (i, j)),
        out_shape=jax.ShapeDtypeStruct((m, n), F32),
        scratch_shapes=[pltpu.VMEM((k, tn), BF16)],
        compiler_params=_cparams(("arbitrary", "arbitrary")),
        name="in_proj",
    )(hn, w_in_all)


MERGE_W_CHUNK = 512


def _merge_kernel(oap_ref, obp_ref, ocp_ref, oas_ref, obs_ref, ocs_ref, ga_ref, gb_ref, gc_ref, w_hbm,
                  o_ref, wbf, stage, sem, *, layer, n_ctx_tiles):
    d = oap_ref.shape[1]
    tn = o_ref.shape[1]
    ck = stage.shape[1]
    n_ck = wbf.shape[0] // ck

    @pl.when(pl.program_id(1) == 0)
    def _():
        col0 = pl.multiple_of(pl.program_id(0) * tn, tn)

        def copy(c, slot):
            r0 = pl.multiple_of(c * ck, ck)
            return pltpu.make_async_copy(w_hbm.at[layer, pl.ds(r0, ck), pl.ds(col0, tn)],
                                         stage.at[slot], sem.at[slot])

        copy(0, 0).start()

        def body(c, carry):
            slot = c & 1

            @pl.when(c + 1 < n_ck)
            def _():
                copy(c + 1, 1 - slot).start()

            copy(c, slot).wait()
            r0 = pl.multiple_of(c * ck, ck)
            wbf[pl.ds(r0, ck), :] = stage[slot].astype(BF16)
            return carry

        lax.fori_loop(0, n_ck, body, 0)

    def branch(o_ref_, g_ref_, r):
        gate = 0.5 * jnp.tanh(0.5 * g_ref_[...]) + 0.5
        return gate * jnp.dot(o_ref_[...], wbf[r * d:(r + 1) * d, :], preferred_element_type=F32)

    def merge(oa_ref, ob_ref, oc_ref):
        acc = branch(oa_ref, ga_ref, 0) + branch(ob_ref, gb_ref, 1) + branch(oc_ref, gc_ref, 2)
        o_ref[...] = acc.astype(o_ref.dtype)

    @pl.when(pl.program_id(1) < n_ctx_tiles)
    def _():
        merge(oap_ref, obp_ref, ocp_ref)

    @pl.when(pl.program_id(1) >= n_ctx_tiles)
    def _():
        merge(oas_ref, obs_ref, ocs_ref)


def _merge(mix_ctx, mix_lat, proj, gate_off, w_branch_all, layer):
    n_p, d = mix_ctx[0].shape
    m = n_p + mix_lat[0].shape[0]
    n = w_branch_all.shape[2]
    tn = _largest_tile(math.gcd(n, gate_off), (512, 256))
    tm = TOKEN_TILE
    ck = MERGE_W_CHUNK
    assert (3 * d) % ck == 0 and d % ck == 0 and n_p % tm == 0
    gb = gate_off // tn
    nb = n // tn
    npt = n_p // tm
    cspec = pl.BlockSpec((tm, d), lambda j, i: (jnp.minimum(i, npt - 1), 0))
    lspec = pl.BlockSpec((tm, d), lambda j, i: (jnp.maximum(i - npt, 0), 0))
    gspec = lambda r: pl.BlockSpec((tm, tn), lambda j, i: (i, gb + r * nb + j))
    return pl.pallas_call(
        functools.partial(_merge_kernel, layer=layer, n_ctx_tiles=npt),
        grid=(nb, m // tm),
        in_specs=[cspec, cspec, cspec, lspec, lspec, lspec, gspec(0), gspec(1), gspec(2),
                  pl.BlockSpec(memory_space=pl.ANY)],
        out_specs=pl.BlockSpec((tm, tn), lambda j, i: (i, j)),
        out_shape=jax.ShapeDtypeStruct((m, n), BF16),
        scratch_shapes=[pltpu.VMEM((3 * d, tn), BF16), pltpu.VMEM((2, ck, tn), F32),
                        pltpu.SemaphoreType.DMA((2,))],
        compiler_params=_cparams(("arbitrary", "arbitrary")),
        name="branch_merge",
    )(*mix_ctx, *mix_lat, proj, proj, proj, w_branch_all)


def _out_proj_kernel(h_ref, w_ref, x_ref, mod_ref, o_ref, wbf_ref):
    @pl.when(pl.program_id(1) == 0)
    def _():
        _cast_rows(w_ref, wbf_ref, 512)

    mix = jnp.dot(h_ref[...], wbf_ref[...], preferred_element_type=F32)
    o_ref[...] = x_ref[...] + mod_ref[2:3, :] * mix


def _out_proj(merged, w_out_all, layer, x, mod_l, row_map):
    m, k = merged.shape
    n = w_out_all.shape[2]
    tn = _largest_tile(n, (1024, 512, 256))
    tm = TOKEN_TILE
    return pl.pallas_call(
        _out_proj_kernel,
        grid=(n // tn, m // tm),
        in_specs=[
            pl.BlockSpec((tm, k), lambda j, i: (i, 0)),
            pl.BlockSpec((None, k, tn), lambda j, i: (layer, 0, j), pipeline_mode=pl.Buffered(1)),
            pl.BlockSpec((tm, tn), lambda j, i: (i, j)),
            pl.BlockSpec((None, 6, tn), lambda j, i: (row_map(i), 0, j)),
        ],
        out_specs=pl.BlockSpec((tm, tn), lambda j, i: (i, j)),
        out_shape=jax.ShapeDtypeStruct((m, n), F32),
        scratch_shapes=[pltpu.VMEM((k, tn), BF16)],
        compiler_params=_cparams(("arbitrary", "arbitrary")),
        name="out_proj",
    )(merged, w_out_all, x, mod_l)


RG_CHUNK = 128


def _rglru_kernel(rx_ref, rg_ref, cw_ref, cb_ref, waf_ref, wxf_ref, wab_ref, wxb_ref,
                  ba_ref, bx_ref, lam_ref, h0_ref, o_ref, st_ref,
                  af, uf, ab, ub, wbf, *, seq):
    tc = RG_CHUNK
    n_chunks = seq // tc
    c = rx_ref.shape[1]
    wbf[0] = waf_ref[...].astype(BF16)
    wbf[1] = wxf_ref[...].astype(BF16)
    wbf[2] = wab_ref[...].astype(BF16)
    wbf[3] = wxb_ref[...].astype(BF16)
    cw = cw_ref[...]
    cb = cb_ref[...]

    def softplus(z):
        return jnp.maximum(z, 0.0) + jnp.log1p(jnp.exp(-jnp.abs(z)))

    sp_f = softplus(-lam_ref[0:1, :])
    sp_b = softplus(-lam_ref[1:2, :])
    row8 = lax.broadcasted_iota(jnp.int32, (tc, c), 0) & (SUBLANES - 1)

    def prep(k, carry):
        r0 = pl.multiple_of(k * tc, tc)
        cur = rx_ref[pl.ds(r0, tc), :]
        p0 = pl.multiple_of(jnp.maximum(r0 - SUBLANES, 0), SUBLANES)
        n0 = pl.multiple_of(jnp.minimum(r0 + tc, seq - SUBLANES), SUBLANES)
        prev8 = jnp.where(k > 0, rx_ref[pl.ds(p0, SUBLANES), :], 0.0)
        next8 = jnp.where(k < n_chunks - 1, rx_ref[pl.ds(n0, SUBLANES), :], 0.0)
        ext = jnp.concatenate([prev8, cur, next8], axis=0)
        ne = tc + 2 * SUBLANES
        xm2 = pltpu.roll(ext, 2, 0)[SUBLANES:SUBLANES + tc]
        xm1 = pltpu.roll(ext, 1, 0)[SUBLANES:SUBLANES + tc]
        xp1 = pltpu.roll(ext, ne - 1, 0)[SUBLANES:SUBLANES + tc]
        xc = cb + xm2 * cw[0:1] + xm1 * cw[1:2] + cur * cw[2:3] + xp1 * cw[3:4]
        xcb = xc.astype(BF16)

        def gates(wi, d, sp):
            r = _sigmoid(jnp.dot(xcb, wbf[wi], preferred_element_type=F32) + ba_ref[d:d + 1, :])
            i = _sigmoid(jnp.dot(xcb, wbf[wi + 1], preferred_element_type=F32) + bx_ref[d:d + 1, :])
            log_a = (-RG_C) * r * sp
            a = jnp.exp(log_a)
            u = jnp.sqrt(1.0 - a * a) * i * xc
            return a, u

        a, u = gates(0, 0, sp_f)
        for dd in (1, 2, 4):
            msk = row8 >= dd
            a_s = jnp.where(msk, _roll8(a, dd), 1.0)
            u_s = jnp.where(msk, _roll8(u, dd), 0.0)
            u = a * u_s + u
            a = a * a_s
        af[pl.ds(r0, tc), :] = a
        uf[pl.ds(r0, tc), :] = u

        a, u = gates(2, 1, sp_b)
        for dd in (1, 2, 4):
            msk = row8 < SUBLANES - dd
            a_s = jnp.where(msk, _roll8(a, SUBLANES - dd), 1.0)
            u_s = jnp.where(msk, _roll8(u, SUBLANES - dd), 0.0)
            u = a * u_s + u
            a = a * a_s
        ab[pl.ds(r0, tc), :] = a
        ub[pl.ds(r0, tc), :] = u
        return carry

    lax.fori_loop(0, n_chunks, prep, 0)

    n8 = seq // SUBLANES

    def carry_step(k, carry):
        hf, hb = carry
        i0 = pl.multiple_of(k * SUBLANES, SUBLANES)
        hf_t = af[pl.ds(i0, SUBLANES), :] * hf + uf[pl.ds(i0, SUBLANES), :]
        uf[pl.ds(i0, SUBLANES), :] = hf_t
        j0 = pl.multiple_of((n8 - 1 - k) * SUBLANES, SUBLANES)
        hb_t = ab[pl.ds(j0, SUBLANES), :] * hb + ub[pl.ds(j0, SUBLANES), :]
        ub[pl.ds(j0, SUBLANES), :] = hb_t
        return hf_t[SUBLANES - 1:SUBLANES, :], hb_t[0:1, :]

    hf, hb = lax.fori_loop(0, n8, carry_step, (h0_ref[0:1, :], h0_ref[1:2, :]))
    st_ref[0:1, :] = hf
    st_ref[1:2, :] = hb

    def finish(k, carry):
        r0 = pl.multiple_of(k * tc, tc)
        hsum = uf[pl.ds(r0, tc), :] + ub[pl.ds(r0, tc), :]
        o_ref[pl.ds(r0, tc), :] = (hsum * _gelu_tanh(rg_ref[pl.ds(r0, tc), :])).astype(o_ref.dtype)
        return carry

    lax.fori_loop(0, n_chunks, finish, 0)


def _rglru(proj, row_blk0, n_seq, seq, d, conv_w, conv_b, rg_wa, rg_wx, rg_ba, rg_bx, rg_lambda,
           h0, layer):
    nb, c = rg_wa.shape[2], rg_wa.shape[3]
    kern = functools.partial(_rglru_kernel, seq=seq)
    wspec = lambda direction: pl.BlockSpec((None, None, None, c, c),
                                           lambda b, n: (layer, direction, n, 0, 0))
    vec2 = pl.BlockSpec((None, 2, c), lambda b, n: (layer, 0, n))
    return pl.pallas_call(
        kern,
        grid=(n_seq, nb),
        in_specs=[
            pl.BlockSpec((seq, c), lambda b, n: (row_blk0 + b, n)),
            pl.BlockSpec((seq, c), lambda b, n: (row_blk0 + b, nb + n)),
            pl.BlockSpec((None, conv_w.shape[1], c), lambda b, n: (layer, 0, n)),
            pl.BlockSpec((None, 1, c), lambda b, n: (layer, 0, n)),
            wspec(0), wspec(0), wspec(1), wspec(1),
            vec2, vec2, vec2,
            pl.BlockSpec((None, 2, c), lambda b, n: (b, 0, n)),
        ],
        out_specs=[
            pl.BlockSpec((seq, c), lambda b, n: (b, n)),
            pl.BlockSpec((None, 2, c), lambda b, n: (b, 0, n)),
        ],
        out_shape=[
            jax.ShapeDtypeStruct((n_seq * seq, d), BF16),
            jax.ShapeDtypeStruct((n_seq, 2, d), F32),
        ],
        scratch_shapes=[pltpu.VMEM((seq, c), F32)] * 4 + [pltpu.VMEM((4, c, c), BF16)],
        compiler_params=_cparams(("arbitrary", "arbitrary")),
        name="rglru",
    )(proj, proj, conv_w, conv_b, rg_wa, rg_wx, rg_wa, rg_wx, rg_ba, rg_bx, rg_lambda, h0)


def _rope(x, cos_f, sin_f):
    return x * cos_f + pltpu.roll(x, HEAD_DIM // 2, 1) * sin_f


def _sink_softmax_pv(s, sink_col, v):
    m = jnp.maximum(jnp.max(s, axis=-1, keepdims=True), sink_col)
    p = jnp.exp(s - m)
    denom = jnp.sum(p, axis=-1, keepdims=True) + jnp.exp(sink_col - m)
    o = jnp.dot(p.astype(BF16), v, preferred_element_type=F32)
    return o / denom


def _sink_col(sink_ref, layer, hkv, group, rows_per_head):
    rows = group * rows_per_head
    assert rows_per_head & (rows_per_head - 1) == 0
    head = lax.broadcasted_iota(jnp.int32, (rows, 1), 0) >> (rows_per_head.bit_length() - 1)
    col = jnp.zeros((rows, 1), F32)
    for g in range(group):
        col = jnp.where(head == g, sink_ref[layer, hkv * group + g], col)
    return col


def _win_ctx_kernel(sink_ref, q_ref, k_ref, v_ref, o_ref, *, layer, group):
    hkv = pl.program_id(1)
    tq = q_ref.shape[0]
    q = jnp.concatenate([q_ref[:, g * HEAD_DIM:(g + 1) * HEAD_DIM] * ATTN_SCALE
                         for g in range(group)], axis=0)
    s = lax.dot_general(q.astype(BF16), k_ref[...].astype(BF16), _DN_QKT,
                        preferred_element_type=F32)
    o = _sink_softmax_pv(s, _sink_col(sink_ref, layer, hkv, group, tq), v_ref[...].astype(BF16))
    for g in range(group):
        o_ref[:, g * HEAD_DIM:(g + 1) * HEAD_DIM] = o[g * tq:(g + 1) * tq].astype(o_ref.dtype)


def _win_ctx(proj, n_seq, seq, q_off, k_off, v_off, n_kv, group, win_sink, layer):
    gw = group * HEAD_DIM
    kern = functools.partial(_win_ctx_kernel, layer=layer, group=group)
    return pl.pallas_call(
        kern,
        grid_spec=pltpu.PrefetchScalarGridSpec(
            num_scalar_prefetch=1,
            grid=(n_seq, n_kv),
            in_specs=[
                pl.BlockSpec((seq, gw), lambda b, h, s: (b, q_off // gw + h)),
                pl.BlockSpec((seq, HEAD_DIM), lambda b, h, s: (b, k_off // HEAD_DIM + h)),
                pl.BlockSpec((seq, HEAD_DIM), lambda b, h, s: (b, v_off // HEAD_DIM + h)),
            ],
            out_specs=pl.BlockSpec((seq, gw), lambda b, h, s: (b, h)),
        ),
        out_shape=jax.ShapeDtypeStruct((n_seq * seq, n_kv * gw), BF16),
        compiler_params=_cparams(("arbitrary", "arbitrary")),
        name="win_attn_ctx",
    )(win_sink, proj, proj, proj)


def _win_lat_kernel(sink_ref, q_ref, kp_ref, kc_ref, kn_ref, vp_ref, vc_ref, vn_ref, ck_ref, cv_ref,
                    cq_ref, sq_ref, cp_ref, sp_ref, cc_ref, sc_ref, cn_ref, sn_ref, o_ref, *,
                    layer, group, n_qblk):
    hkv = pl.program_id(1)
    qi = pl.program_id(2)
    tq = q_ref.shape[0]
    past = ck_ref.shape[0]
    cq, sq = cq_ref[...], sq_ref[...]
    q = jnp.concatenate([_rope(q_ref[:, g * HEAD_DIM:(g + 1) * HEAD_DIM], cq, sq) * ATTN_SCALE
                         for g in range(group)], axis=0).astype(BF16)
    k = jnp.concatenate([
        ck_ref[...].astype(BF16),
        _rope(kp_ref[...], cp_ref[...], sp_ref[...]).astype(BF16),
        _rope(kc_ref[...], cc_ref[...], sc_ref[...]).astype(BF16),
        _rope(kn_ref[...], cn_ref[...], sn_ref[...]).astype(BF16)], axis=0)
    v = jnp.concatenate([cv_ref[...].astype(BF16), vp_ref[...].astype(BF16),
                         vc_ref[...].astype(BF16), vn_ref[...].astype(BF16)], axis=0)
    s = lax.dot_general(q, k, _DN_QKT, preferred_element_type=F32)
    r = lax.broadcasted_iota(jnp.int32, s.shape, 0) & (tq - 1)
    c = lax.broadcasted_iota(jnp.int32, s.shape, 1) - past
    in_prev = (c >= 0) & (c < tq)
    in_next = c >= 2 * tq
    bad = (in_prev & ((c < r + (tq - WINDOW)) | (qi == 0))) | \
          (in_next & ((c - 2 * tq > r - (tq - WINDOW)) | (qi == n_qblk - 1)))
    s = jnp.where(bad, NEG_INF, s)
    o = _sink_softmax_pv(s, _sink_col(sink_ref, layer, hkv, group, tq), v)
    for g in range(group):
        o_ref[:, g * HEAD_DIM:(g + 1) * HEAD_DIM] = o[g * tq:(g + 1) * tq].astype(o_ref.dtype)


def _win_lat(proj, row0, n_seq, seq, q_off, k_off, v_off, n_kv, group, win_sink, layer,
             cache_k, cache_v, cos_f, sin_f):
    tq = WINDOW
    assert seq % tq == 0 and row0 % tq == 0
    nq = seq // tq
    gw = group * HEAD_DIM
    past = cache_k.shape[2]
    rb0 = row0 // tq
    kb, vb = k_off // HEAD_DIM, v_off // HEAD_DIM
    kern = functools.partial(_win_lat_kernel, layer=layer, group=group, n_qblk=nq)
    prv = lambda i: jnp.maximum(i - 1, 0)
    nxt = lambda i: jnp.minimum(i + 1, nq - 1)
    kv = lambda colb, f: pl.BlockSpec((tq, HEAD_DIM),
                                      lambda b, h, i, s: (rb0 + b * nq + f(i), colb + h))
    cache = pl.BlockSpec((None, None, past, HEAD_DIM), lambda b, h, i, s: (b, layer, 0, h))
    tab = lambda f: pl.BlockSpec((tq, HEAD_DIM), lambda b, h, i, s: (f(i), 0))
    same = lambda i: i
    return pl.pallas_call(
        kern,
        grid_spec=pltpu.PrefetchScalarGridSpec(
            num_scalar_prefetch=1,
            grid=(n_seq, n_kv, nq),
            in_specs=[
                pl.BlockSpec((tq, gw), lambda b, h, i, s: (rb0 + b * nq + i, q_off // gw + h)),
                kv(kb, prv), kv(kb, same), kv(kb, nxt),
                kv(vb, prv), kv(vb, same), kv(vb, nxt),
                cache, cache,
                tab(same), tab(same), tab(prv), tab(prv), tab(same), tab(same), tab(nxt), tab(nxt),
            ],
            out_specs=pl.BlockSpec((tq, gw), lambda b, h, i, s: (b * nq + i, h)),
        ),
        out_shape=jax.ShapeDtypeStruct((n_seq * seq, n_kv * gw), BF16),
        compiler_params=_cparams(("arbitrary", "arbitrary", "arbitrary")),
        name="win_attn_lat",
    )(win_sink, proj, proj, proj, proj, proj, proj, proj, cache_k, cache_v,
      cos_f, sin_f, cos_f, sin_f, cos_f, sin_f, cos_f, sin_f)


def _diff_lambda(l4, lam_init):
    t1 = jnp.sum(l4[0:1, :] * l4[1:2, :], axis=-1, keepdims=True)
    t2 = jnp.sum(l4[2:3, :] * l4[3:4, :], axis=-1, keepdims=True)
    return jnp.exp(t1) - jnp.exp(t2) + lam_init


ATTN_SCALE = HEAD_DIM ** -0.5
_DN_QKT = (((1,), (1,)), ((), ()))


def _subln(o, g, lam_init):
    ms = jnp.mean(o * o, axis=-1, keepdims=True)
    return o * lax.rsqrt(ms + EPS) * g * (1.0 - lam_init)


def _diff_core(q0, q1, k0, k1, v, lam, g, lam_init):
    s0 = lax.dot_general(q0, k0, _DN_QKT, preferred_element_type=F32)
    s1 = lax.dot_general(q1, k1, _DN_QKT, preferred_element_type=F32)
    e0 = jnp.exp(s0 - jnp.max(s0, axis=-1, keepdims=True))
    e1 = jnp.exp(s1 - jnp.max(s1, axis=-1, keepdims=True))
    c0 = 1.0 / jnp.sum(e0, axis=-1, keepdims=True)
    c1 = lam / jnp.sum(e1, axis=-1, keepdims=True)
    a = e0 * c0 - e1 * c1
    o = jnp.dot(a.astype(BF16), v, preferred_element_type=F32)
    return _subln(o, g, lam_init)


def _diff_ctx_kernel(q_ref, k_ref, v_ref, l4_ref, g_ref, o_ref, *, lam_init):
    lam = _diff_lambda(l4_ref[...], lam_init)
    hd = HEAD_DIM
    for h in range(q_ref.shape[1] // (2 * hd)):
        c0, c1, c2 = 2 * h * hd, (2 * h + 1) * hd, (2 * h + 2) * hd
        o = _diff_core((q_ref[:, c0:c1] * ATTN_SCALE).astype(BF16),
                       (q_ref[:, c1:c2] * ATTN_SCALE).astype(BF16),
                       k_ref[:, c0:c1].astype(BF16), k_ref[:, c1:c2].astype(BF16),
                       v_ref[:, c0:c2].astype(BF16), lam, g_ref[...], lam_init)
        o_ref[:, c0:c2] = o.astype(o_ref.dtype)


DIFF_CTX_HEADS_PER_STEP = 2


def _diff_ctx(proj, n_seq, seq, q_off, k_off, v_off, n_heads, diff_lambda, subln_g, layer, lam_init):
    hps = DIFF_CTX_HEADS_PER_STEP if n_heads % DIFF_CTX_HEADS_PER_STEP == 0 else 1
    hw = 2 * HEAD_DIM
    bw = hps * hw
    kern = functools.partial(_diff_ctx_kernel, lam_init=lam_init)
    blk = lambda off: pl.BlockSpec((seq, bw), lambda b, h: (b, off // bw + h))
    assert q_off % bw == 0 and k_off % bw == 0 and v_off % bw == 0
    return pl.pallas_call(
        kern,
        grid=(n_seq, n_heads // hps),
        in_specs=[blk(q_off), blk(k_off), blk(v_off),
                  pl.BlockSpec((None, 4, HEAD_DIM), lambda b, h: (layer, 0, 0)),
                  pl.BlockSpec((None, 1, hw), lambda b, h: (layer, 0, 0))],
        out_specs=pl.BlockSpec((seq, bw), lambda b, h: (b, h)),
        out_shape=jax.ShapeDtypeStruct((n_seq * seq, n_heads * hw), BF16),
        compiler_params=_cparams(("arbitrary", "arbitrary")),
        name="diff_attn_ctx",
    )(proj, proj, proj, diff_lambda, subln_g)


DIFF_KV_CHUNK = 256


def _diff_lat_kernel(q_ref, k_ref, v_ref, ck_ref, cv_ref, cq_ref, sq_ref, call_ref, sall_ref,
                     l4_ref, g_ref, o_ref, kall, vall, *, lam_init):
    hd = HEAD_DIM
    past = ck_ref.shape[0]
    seq = k_ref.shape[0]

    @pl.when(pl.program_id(2) == 0)
    def _():
        kall[0:past, :] = ck_ref[...].astype(BF16)
        vall[0:past, :] = cv_ref[...].astype(BF16)

        def fill(i, carry):
            r0 = pl.multiple_of(i * DIFF_KV_CHUNK, DIFF_KV_CHUNK)
            rows = pl.ds(r0, DIFF_KV_CHUNK)
            dst = pl.ds(past + r0, DIFF_KV_CHUNK)
            cs, sn = call_ref[rows, :], sall_ref[rows, :]
            kall[dst, 0:hd] = _rope(k_ref[rows, 0:hd], cs, sn).astype(BF16)
            kall[dst, hd:2 * hd] = _rope(k_ref[rows, hd:2 * hd], cs, sn).astype(BF16)
            vall[dst, :] = v_ref[rows, :].astype(BF16)
            return carry

        lax.fori_loop(0, seq // DIFF_KV_CHUNK, fill, 0)

    lam = _diff_lambda(l4_ref[...], lam_init)
    cq, sq = cq_ref[...], sq_ref[...]
    q0 = (_rope(q_ref[:, 0:hd], cq, sq) * ATTN_SCALE).astype(BF16)
    q1 = (_rope(q_ref[:, hd:2 * hd], cq, sq) * ATTN_SCALE).astype(BF16)
    o = _diff_core(q0, q1, kall[:, 0:hd], kall[:, hd:2 * hd], vall[...], lam, g_ref[...], lam_init)
    o_ref[...] = o.astype(o_ref.dtype)


def _diff_lat(proj, row0, n_seq, seq, q_off, k_off, v_off, n_heads, diff_lambda, subln_g, layer,
              lam_init, cache_k, cache_v, cos_f, sin_f):
    hw = 2 * HEAD_DIM
    tq = 256
    kc = DIFF_KV_CHUNK
    assert seq % kc == 0 and row0 % seq == 0 and seq % tq == 0
    nq = seq // tq
    past = cache_k.shape[2]
    sb0 = row0 // seq
    kern = functools.partial(_diff_lat_kernel, lam_init=lam_init)
    full = lambda off: pl.BlockSpec((seq, hw), lambda b, h, i: (sb0 + b, off // hw + h))
    cache = pl.BlockSpec((None, None, past, hw), lambda b, h, i: (b, layer, 0, h))
    return pl.pallas_call(
        kern,
        grid=(n_seq, n_heads, nq),
        in_specs=[
            pl.BlockSpec((tq, hw), lambda b, h, i: (sb0 * nq + b * nq + i, q_off // hw + h)),
            full(k_off), full(v_off), cache, cache,
            pl.BlockSpec((tq, HEAD_DIM), lambda b, h, i: (i, 0)),
            pl.BlockSpec((tq, HEAD_DIM), lambda b, h, i: (i, 0)),
            pl.BlockSpec((seq, HEAD_DIM), lambda b, h, i: (0, 0)),
            pl.BlockSpec((seq, HEAD_DIM), lambda b, h, i: (0, 0)),
            pl.BlockSpec((None, 4, HEAD_DIM), lambda b, h, i: (layer, 0, 0)),
            pl.BlockSpec((None, 1, hw), lambda b, h, i: (layer, 0, 0)),
        ],
        out_specs=pl.BlockSpec((tq, hw), lambda b, h, i: (b * nq + i, h)),
        out_shape=jax.ShapeDtypeStruct((n_seq * seq, n_heads * hw), BF16),
        scratch_shapes=[pltpu.VMEM((past + seq, hw), BF16), pltpu.VMEM((past + seq, hw), BF16)],
        compiler_params=_cparams(("arbitrary", "arbitrary", "arbitrary")),
        name="diff_attn_lat",
    )(proj, proj, proj, cache_k, cache_v, cos_f, sin_f, cos_f, sin_f, diff_lambda, subln_g)


def _moe_plan(eid, n_experts, rows_per_blk):
    n = eid.shape[0]
    a = n * TOP_K
    flat_e = eid.reshape(-1)
    onehot = (flat_e[:, None] == jnp.arange(n_experts, dtype=jnp.int32)[None, :]).astype(jnp.int32)
    csum = jnp.cumsum(onehot, axis=0)
    rank = jnp.take_along_axis(csum, flat_e[:, None], axis=1)[:, 0] - 1
    counts = csum[-1]
    nblk_e = (counts + rows_per_blk - 1) // rows_per_blk
    blk_end = jnp.cumsum(nblk_e)
    blk_start = blk_end - nblk_e
    dest = (blk_start[flat_e] * rows_per_blk + rank).astype(jnp.int32)
    n_blk = a // rows_per_blk + n_experts
    n_used = blk_end[-1].astype(jnp.int32)
    slot_tok = jnp.zeros((n_blk * rows_per_blk,), jnp.int32).at[dest].set(
        jnp.arange(a, dtype=jnp.int32) // TOP_K)
    blk_ids = jnp.arange(n_blk, dtype=jnp.int32)
    blk_e = jnp.minimum(jnp.searchsorted(blk_end, blk_ids, side='right'), n_experts - 1)
    blk_e = jnp.where(blk_ids < n_used, blk_e, blk_e[jnp.maximum(n_used - 1, 0)]).astype(jnp.int32)
    first = jnp.concatenate([jnp.ones((1,), jnp.int32),
                             (blk_e[1:] != blk_e[:-1]).astype(jnp.int32)])
    return slot_tok, dest, blk_e, first, n_used.reshape(1)


def _gather_rows_kernel(tok_ref, x_hbm, o_ref, buf, sem, *, rows_per_step):
    i = pl.program_id(0)
    slot = i & 1

    def issue(step, dst_slot):
        base = step * rows_per_step

        def body(r, carry):
            tok = tok_ref[base + r]
            pltpu.make_async_copy(x_hbm.at[pl.ds(tok, 1)], buf.at[dst_slot, pl.ds(r, 1)],
                                  sem.at[dst_slot]).start()
            return carry

        lax.fori_loop(0, rows_per_step, body, 0, unroll=DMA_ISSUE_UNROLL)

    @pl.when(i == 0)
    def _():
        issue(0, 0)

    @pl.when(i + 1 < pl.num_programs(0))
    def _():
        issue(i + 1, 1 - slot)

    pltpu.make_async_copy(x_hbm.at[pl.ds(0, rows_per_step)], buf.at[slot], sem.at[slot]).wait()
    w = buf.shape[2]
    lo, hi = _unpack_bf16_pairs(buf[slot])
    o_ref[:, 0:w] = lo.astype(o_ref.dtype)
    o_ref[:, w:2 * w] = hi.astype(o_ref.dtype)


def _gather_rows(x, slot_tok):
    rows = slot_tok.shape[0]
    w = x.shape[1]
    d = 2 * w
    rps = MOE_ROWS
    kern = functools.partial(_gather_rows_kernel, rows_per_step=rps)
    return pl.pallas_call(
        kern,
        grid_spec=pltpu.PrefetchScalarGridSpec(
            num_scalar_prefetch=1,
            grid=(rows // rps,),
            in_specs=[pl.BlockSpec(memory_space=pl.ANY)],
            out_specs=pl.BlockSpec((rps, d), lambda i, tok: (i, 0)),
            scratch_shapes=[pltpu.VMEM((2, rps, w), x.dtype), pltpu.SemaphoreType.DMA((2,))],
        ),
        out_shape=jax.ShapeDtypeStruct((rows, d), BF16),
        compiler_params=_cparams(("arbitrary",)),
        name="moe_gather",
    )(slot_tok, x)


def _gmm1_kernel(be_ref, first_ref, nused_ref, x_ref, wg_ref, wu_ref, o_ref, wgb, wub):
    blk = pl.program_id(1)

    @pl.when(first_ref[blk] == 1)
    def _():
        _cast_rows(wg_ref, wgb, 512)
        _cast_rows(wu_ref, wub, 512)

    @pl.when(blk < nused_ref[0])
    def _():
        x = x_ref[...]
        g = jnp.dot(x, wgb[...], preferred_element_type=F32)
        u = jnp.dot(x, wub[...], preferred_element_type=F32)
        o_ref[...] = (g * _sigmoid(g) * u).astype(o_ref.dtype)

    @pl.when(blk >= nused_ref[0])
    def _():
        o_ref[...] = jnp.zeros(o_ref.shape, o_ref.dtype)


def _gmm1(xs, w_gate_all, w_up_all, layer, blk_e, first, n_used):
    rows, d = xs.shape
    de = w_gate_all.shape[3]
    tn = _largest_tile(de, (512, 256, 128))
    r = MOE_ROWS
    wspec = pl.BlockSpec((None, None, d, tn), lambda j, b, be, fi, nu: (layer, be[b], 0, j))
    return pl.pallas_call(
        _gmm1_kernel,
        grid_spec=pltpu.PrefetchScalarGridSpec(
            num_scalar_prefetch=3,
            grid=(de // tn, rows // r),
            in_specs=[pl.BlockSpec((r, d), lambda j, b, be, fi, nu: (b, 0)), wspec, wspec],
            out_specs=pl.BlockSpec((r, tn), lambda j, b, be, fi, nu: (b, j)),
            scratch_shapes=[pltpu.VMEM((d, tn), BF16), pltpu.VMEM((d, tn), BF16)],
        ),
        out_shape=jax.ShapeDtypeStruct((rows, de), BF16),
        compiler_params=_cparams(("arbitrary", "arbitrary")),
        name="moe_gate_up",
    )(blk_e, first, n_used, xs, w_gate_all, w_up_all)


def _gmm2_kernel(be_ref, first_ref, nused_ref, h_ref, wd_ref, o_ref, wdb):
    blk = pl.program_id(1)

    @pl.when(first_ref[blk] == 1)
    def _():
        _cast_rows(wd_ref, wdb, 512)

    @pl.when(blk < nused_ref[0])
    def _():
        o_ref[...] = _pack_bf16_pairs(jnp.dot(h_ref[...], wdb[...], preferred_element_type=F32))

    @pl.when(blk >= nused_ref[0])
    def _():
        o_ref[...] = jnp.zeros(o_ref.shape, o_ref.dtype)


def _gmm2(h, w_down_all, layer, blk_e, first, n_used):
    rows, de = h.shape
    d = w_down_all.shape[3]
    tn = _largest_tile(d, (2048, 1024, 512, 256))
    r = MOE_ROWS
    ys = pl.pallas_call(
        _gmm2_kernel,
        grid_spec=pltpu.PrefetchScalarGridSpec(
            num_scalar_prefetch=3,
            grid=(d // tn, rows // r),
            in_specs=[
                pl.BlockSpec((r, de), lambda j, b, be, fi, nu: (b, 0)),
                pl.BlockSpec((None, None, de, tn), lambda j, b, be, fi, nu: (layer, be[b], 0, j)),
            ],
            out_specs=pl.BlockSpec((r, tn // 2), lambda j, b, be, fi, nu: (b, j)),
            scratch_shapes=[pltpu.VMEM((de, tn), BF16)],
        ),
        out_shape=jax.ShapeDtypeStruct((rows, d // 2), jnp.uint32),
        compiler_params=_cparams(("arbitrary", "arbitrary")),
        name="moe_down",
    )(blk_e, first, n_used, h, w_down_all)
    return ys, tn


def _combine_kernel(slot_ref, x_ref, wts_ref, mod_ref, ys_hbm, o_ref, buf, sem, *, pack_cols):
    tt = x_ref.shape[0]
    base = pl.program_id(0) * tt

    def issue(r, carry):
        for k in range(TOP_K):
            s = slot_ref[TOP_K * (base + r) + k]
            pltpu.make_async_copy(ys_hbm.at[pl.ds(s, 1)], buf.at[k, pl.ds(r, 1)], sem).start()
        return carry

    lax.fori_loop(0, tt, issue, 0, unroll=DMA_ISSUE_UNROLL)
    for k in range(TOP_K):
        pltpu.make_async_copy(ys_hbm.at[pl.ds(0, tt)], buf.at[k], sem).wait()
    w = wts_ref[...]
    w0, w1 = w[:, 0:1], w[:, 1:2]
    half = pack_cols // 2
    for j in range(x_ref.shape[1] // pack_cols):
        words = slice(j * half, (j + 1) * half)
        lo0, hi0 = _unpack_bf16_pairs(buf[0, :, words])
        lo1, hi1 = _unpack_bf16_pairs(buf[1, :, words])
        for y, c0 in ((w0 * lo0 + w1 * lo1, j * pack_cols), (w0 * hi0 + w1 * hi1, j * pack_cols + half)):
            cols = slice(c0, c0 + half)
            o_ref[:, cols] = x_ref[:, cols] + mod_ref[5:6, cols] * y


def _combine(x, wts, mod_l, row_map, ys, slots, pack_cols):
    n, d = x.shape
    tt = TOKEN_TILE
    return pl.pallas_call(
        functools.partial(_combine_kernel, pack_cols=pack_cols),
        grid_spec=pltpu.PrefetchScalarGridSpec(
            num_scalar_prefetch=1,
            grid=(n // tt,),
            in_specs=[
                pl.BlockSpec((tt, d), lambda i, s: (i, 0)),
                pl.BlockSpec((tt, LANES), lambda i, s: (i, 0)),
                pl.BlockSpec((None, 6, d), lambda i, s: (row_map(i), 0, 0)),
                pl.BlockSpec(memory_space=pl.ANY),
            ],
            out_specs=pl.BlockSpec((tt, d), lambda i, s: (i, 0)),
            scratch_shapes=[pltpu.VMEM((TOP_K, tt, d // 2), jnp.uint32), pltpu.SemaphoreType.DMA(())],
        ),
        out_shape=jax.ShapeDtypeStruct((n, d), F32),
        compiler_params=_cparams(("arbitrary",)),
        name="moe_combine",
    )(slots, x, wts, mod_l, ys)


def _rope_tables(seq):
    rows = seq // GRID_W
    row = jnp.repeat(jnp.arange(rows), GRID_W).astype(F32)
    col = jnp.tile(jnp.arange(GRID_W), rows).astype(F32)
    n_freq = HEAD_DIM // 4
    inv = 1.0 / (ROPE_BASE ** (jnp.arange(n_freq, dtype=F32) / n_freq))
    ang = jnp.concatenate([row[:, None] * inv, col[:, None] * inv], axis=-1)
    cos, sin = jnp.cos(ang), jnp.sin(ang)
    return jnp.concatenate([cos, cos], axis=-1), jnp.concatenate([-sin, sin], axis=-1)


def kernel(x_prompt, x_sample, c, cache_win_k, cache_win_v, cache_diff_k, cache_diff_v, state_rnn, c_ctx, w_mod, b_mod, norm1_g, norm2_g, final_g, w_in, conv_w, conv_b, rg_wa, rg_ba, rg_wx, rg_bx, rg_lambda, win_sink, diff_lambda, diff_subln_g, w_branch, w_out, moe_w_group, moe_b_group, moe_w_router, moe_b_router, moe_w_gate, moe_w_up, moe_w_down):
    bp, tp, d = x_prompt.shape
    bs, ts, _ = x_sample.shape
    depth = w_in.shape[0]
    past = cache_win_k.shape[2]
    n_kv = cache_win_k.shape[3]
    n_win = win_sink.shape[1]
    group = n_win // n_kv
    n_diff = cache_diff_k.shape[3]
    d_rnn = rg_ba.shape[2]
    n_groups = moe_w_group.shape[2]
    n_experts = moe_w_router.shape[2]
    per_group = n_experts // n_groups
    n_p, n_s = bp * tp, bs * ts
    tt = TOKEN_TILE
    assert tp % tt == 0 and ts % tt == 0 and n_p % ts == 0 and d_rnn == d
    assert n_groups + n_experts <= LANES

    sections = (d_rnn, d_rnn, n_win * HEAD_DIM, n_kv * HEAD_DIM, n_kv * HEAD_DIM,
                n_diff * 2 * HEAD_DIM, n_diff * 2 * HEAD_DIM, n_diff * 2 * HEAD_DIM, 3 * d)
    offs = [0]
    for s_ in sections:
        offs.append(offs[-1] + s_)
    _, _, o_wq, o_wk, o_wv, o_dq, o_dk, o_dv, o_mg = offs[:9]

    row_map = _mod_row_map(n_p // tt, ts // tt)

    n_rows = -(-(1 + bs) // SUBLANES) * SUBLANES
    cvecs = jnp.zeros((n_rows, d), F32).at[0].set(c_ctx).at[1:1 + bs].set(c)
    mod = _ada_mod_all(cvecs, w_mod, b_mod).reshape(depth, n_rows, 6, d)

    cos_f, sin_f = _rope_tables(ts)
    ck_win = cache_win_k.reshape(bs, depth, past, n_kv * HEAD_DIM)
    cv_win = cache_win_v.reshape(bs, depth, past, n_kv * HEAD_DIM)
    ck_diff = cache_diff_k.reshape(bs, depth, past, n_diff * 2 * HEAD_DIM)
    cv_diff = cache_diff_v.reshape(bs, depth, past, n_diff * 2 * HEAD_DIM)
    w_rt = jnp.zeros((depth, d, LANES), F32).at[:, :, :n_groups].set(moe_w_group) \
        .at[:, :, n_groups:n_groups + n_experts].set(moe_w_router)
    b_rt = jnp.zeros((depth, 1, LANES), F32).at[:, 0, :n_groups].set(moe_b_group) \
        .at[:, 0, n_groups:n_groups + n_experts].set(moe_b_router)
    w_rt_hi = w_rt.astype(BF16)
    w_rt_lo = (w_rt - w_rt_hi.astype(F32)).astype(BF16)
    h0_prompt = jnp.zeros((bp, 2, d_rnn), F32)
    norm1_g = norm1_g.reshape(depth, 1, d)
    norm2_g = norm2_g.reshape(depth, 1, d)
    conv_b = conv_b.reshape(depth, 1, d_rnn)
    diff_subln_g = diff_subln_g.reshape(depth, 1, 2 * HEAD_DIM)

    x = jnp.concatenate([x_prompt.reshape(n_p, d), x_sample.reshape(n_s, d)], axis=0)
    new_wk, new_wv, new_dk, new_dv, new_st = [], [], [], [], []
    for l in range(depth):
        lam_init = 0.8 - 0.6 * math.exp(-0.3 * l)
        mod_l = mod[l]
        hn = _norm1(x, norm1_g, l, mod_l, row_map)
        proj = _in_proj(hn, w_in, l)

        rg_args = (conv_w, conv_b, rg_wa, rg_wx, rg_ba, rg_bx, rg_lambda)
        oa_p, st_p = _rglru(proj, 0, bp, tp, d_rnn, *rg_args, h0_prompt, l)
        oa_s, _ = _rglru(proj, n_p // ts, bs, ts, d_rnn, *rg_args, state_rnn[:, l], l)

        ob_p = _win_ctx(proj, bp, tp, o_wq, o_wk, o_wv, n_kv, group, win_sink, l)
        ob_s = _win_lat(proj, n_p, bs, ts, o_wq, o_wk, o_wv, n_kv, group, win_sink, l,
                        ck_win, cv_win, cos_f, sin_f)
        oc_p = _diff_ctx(proj, bp, tp, o_dq, o_dk, o_dv, n_diff, diff_lambda, diff_subln_g, l, lam_init)
        oc_s = _diff_lat(proj, n_p, bs, ts, o_dq, o_dk, o_dv, n_diff, diff_lambda, diff_subln_g, l,
                         lam_init, ck_diff, cv_diff, cos_f, sin_f)

        merged = _merge((oa_p, ob_p, oc_p), (oa_s, ob_s, oc_s), proj, o_mg, w_branch, l)
        x = _out_proj(merged, w_out, l, x, mod_l, row_map)

        hn2, eid, wts = _norm2_router(x, norm2_g, l, mod_l, row_map, w_rt_hi[l], w_rt_lo[l], b_rt[l],
                                      n_groups, per_group)
        slot_tok, dest, blk_e, first, n_used = _moe_plan(eid[:, :TOP_K], n_experts, MOE_ROWS)
        xs = _gather_rows(hn2, slot_tok)
        hmid = _gmm1(xs, moe_w_gate, moe_w_up, l, blk_e, first, n_used)
        ys, pack_cols = _gmm2(hmid, moe_w_down, l, blk_e, first, n_used)
        x = _combine(x, wts, mod_l, row_map, ys, dest, pack_cols)

        pr = proj[:n_p]
        new_wk.append(pr[:, o_wk:o_wv].reshape(bp, tp, n_kv, HEAD_DIM))
        new_wv.append(pr[:, o_wv:o_dq].reshape(bp, tp, n_kv, HEAD_DIM))
        new_dk.append(pr[:, o_dk:o_dv].reshape(bp, tp, n_diff, 2, HEAD_DIM))
        new_dv.append(pr[:, o_dv:o_mg].reshape(bp, tp, n_diff, 2 * HEAD_DIM))
        new_st.append(st_p)

    y = _final_norm(x, final_g)
    return (y[:n_p].reshape(bp, tp, d), y[n_p:].reshape(bs, ts, d),
            jnp.stack(new_wk, axis=1), jnp.stack(new_wv, axis=1),
            jnp.stack(new_dk, axis=1), jnp.stack(new_dv, axis=1), jnp.stack(new_st, axis=1))
```

```python
import functools
import math

import jax
import jax.numpy as jnp
from jax import lax
from jax.experimental import pallas as pl
from jax.experimental.pallas import tpu as pltpu

F32 = jnp.float32
BF16 = jnp.bfloat16

HEAD_DIM = 128
GRID_W = 64
ROPE_BASE = 10000.0
WINDOW = 128
CONV_LEFT = 2
RG_C = 8.0
TOP_K = 2
EPS = 1e-6
NEG_INF = -1e30

V7X_VMEM_LIMIT_BYTES = 56 * 1024 * 1024
LANES = 128
SUBLANES = 8

TOKEN_TILE = 256
MOE_ROWS = 256
DMA_ISSUE_UNROLL = 8


def _cparams(sem):
    return pltpu.CompilerParams(dimension_semantics=sem, vmem_limit_bytes=V7X_VMEM_LIMIT_BYTES)


def _sigmoid(x):
    return 0.5 * jnp.tanh(0.5 * x) + 0.5


def _roll8(x, shift):
    r, c = x.shape
    return pltpu.roll(x.reshape(r // SUBLANES, SUBLANES, c), shift, 1).reshape(r, c)


def _pack_bf16_pairs(x):
    w = x.shape[1] // 2
    lo = lax.bitcast_convert_type(x[:, :w].astype(BF16).astype(F32), jnp.uint32) >> 16
    hi = lax.bitcast_convert_type(x[:, w:].astype(BF16).astype(F32), jnp.uint32) & jnp.uint32(0xFFFF0000)
    return hi | lo


def _unpack_bf16_pairs(p):
    lo = lax.bitcast_convert_type(p << 16, F32)
    hi = lax.bitcast_convert_type(p & jnp.uint32(0xFFFF0000), F32)
    return lo, hi


def _gelu_tanh(x):
    c = math.sqrt(2.0 / math.pi)
    return 0.5 * x * (1.0 + jnp.tanh(c * (x + 0.044715 * (x * x * x))))


def _largest_tile(n, candidates):
    for c in candidates:
        if n % c == 0:
            return c
    raise ValueError(f"no tile in {candidates} divides {n}")


def _cast_rows(src_ref, dst_ref, chunk):
    rows = src_ref.shape[0]
    chunk = min(chunk, rows)

    def body(k, c):
        r0 = pl.multiple_of(k * chunk, chunk)
        dst_ref[pl.ds(r0, chunk), :] = src_ref[pl.ds(r0, chunk), :].astype(dst_ref.dtype)
        return c

    lax.fori_loop(0, rows // chunk, body, 0)


def _adamod_kernel(cv_ref, w_ref, b_ref, o_ref):
    cv = cv_ref[...]
    s = cv * _sigmoid(cv)
    o_ref[...] = jnp.dot(s.astype(BF16), w_ref[...].astype(BF16),
                         preferred_element_type=F32) + b_ref[...]


def _ada_mod_all(cvecs, w_mod, b_mod):
    depth, d, n6 = w_mod.shape
    rows = cvecs.shape[0]
    tn = _largest_tile(n6, (512, 256, 128))
    return pl.pallas_call(
        _adamod_kernel,
        grid=(depth, n6 // tn),
        in_specs=[
            pl.BlockSpec((rows, d), lambda l, j: (0, 0)),
            pl.BlockSpec((None, d, tn), lambda l, j: (l, 0, j)),
            pl.BlockSpec((None, 1, tn), lambda l, j: (l, 0, j)),
        ],
        out_specs=pl.BlockSpec((None, rows, tn), lambda l, j: (l, 0, j)),
        out_shape=jax.ShapeDtypeStruct((depth, rows, n6), F32),
        compiler_params=_cparams(("arbitrary", "arbitrary")),
        name="ada_mod",
    )(cvecs, w_mod, b_mod.reshape(depth, 1, n6))


def _mod_row_map(n_prompt_tiles, tiles_per_latent_seq):
    def row(i):
        return jnp.where(i < n_prompt_tiles, 0, 1 + (i - n_prompt_tiles) // tiles_per_latent_seq)
    return row


def _norm_mod(x, g, mod_ref, shift_row, scale_row):
    ms = jnp.mean(x * x, axis=-1, keepdims=True)
    y = x * lax.rsqrt(ms + EPS) * g
    return y * (1.0 + mod_ref[scale_row:scale_row + 1, :]) + mod_ref[shift_row:shift_row + 1, :]


def _norm1_kernel(x_ref, g_ref, mod_ref, o_ref):
    o_ref[...] = _norm_mod(x_ref[...], g_ref[...], mod_ref, 0, 1).astype(o_ref.dtype)


def _norm1(x, g_all, layer, mod_l, row_map):
    n, d = x.shape
    tt = TOKEN_TILE
    return pl.pallas_call(
        _norm1_kernel,
        grid=(n // tt,),
        in_specs=[
            pl.BlockSpec((tt, d), lambda i: (i, 0)),
            pl.BlockSpec((None, 1, d), lambda i: (layer, 0, 0)),
            pl.BlockSpec((None, 6, d), lambda i: (row_map(i), 0, 0)),
        ],
        out_specs=pl.BlockSpec((tt, d), lambda i: (i, 0)),
        out_shape=jax.ShapeDtypeStruct((n, d), BF16),
        compiler_params=_cparams(("arbitrary",)),
        name="norm1_mod",
    )(x, g_all, mod_l)


def _norm2_router_kernel(x_ref, g_ref, mod_ref, wh_ref, wl_ref, br_ref, hn_ref, eid_ref, wts_ref,
                         cnt_ref, *, n_groups, per_group):
    y = _norm_mod(x_ref[...], g_ref[...], mod_ref, 3, 4)
    hn_ref[...] = _pack_bf16_pairs(y)
    y_hi = y.astype(BF16)
    y_lo = (y - y_hi.astype(F32)).astype(BF16)
    w_hi = wh_ref[...]
    logits = (jnp.dot(y_hi, w_hi, preferred_element_type=F32)
              + (jnp.dot(y_lo, w_hi, preferred_element_type=F32)
                 + jnp.dot(y_hi, wl_ref[...], preferred_element_type=F32))) + br_ref[...]
    lane = lax.broadcasted_iota(jnp.int32, logits.shape, 1).astype(F32)
    big = float(LANES)
    gmask = lane < n_groups
    glog = jnp.where(gmask, logits, NEG_INF)
    gmax = jnp.max(glog, axis=-1, keepdims=True)
    gsum = jnp.sum(jnp.where(gmask, jnp.exp(glog - gmax), 0.0), axis=-1, keepdims=True)
    g_p = 1.0 / gsum
    g_i = jnp.min(jnp.where(gmask & (glog == gmax), lane, big), axis=-1, keepdims=True)
    lo = n_groups + g_i * per_group
    rmask = (lane >= lo) & (lane < lo + per_group)
    rlog = jnp.where(rmask, logits, NEG_INF)
    m1 = jnp.max(rlog, axis=-1, keepdims=True)
    i1 = jnp.min(jnp.where(rmask & (rlog == m1), lane, big), axis=-1, keepdims=True)
    rmask2 = rmask & (lane != i1)
    rlog2 = jnp.where(rmask2, logits, NEG_INF)
    m2 = jnp.max(rlog2, axis=-1, keepdims=True)
    i2 = jnp.min(jnp.where(rmask2 & (rlog2 == m2), lane, big), axis=-1, keepdims=True)
    e = jnp.exp(m2 - m1)
    w1 = g_p / (1.0 + e)
    w2 = w1 * e
    wts_ref[...] = jnp.where(lane == 0, w1, jnp.where(lane == 1, w2, 0.0))
    tt = logits.shape[0]

    @pl.when(pl.program_id(0) == 0)
    def _():
        cnt_ref[...] = jnp.zeros(cnt_ref.shape, F32)

    onehot = jnp.where((lane == i1) | (lane == i2), 1.0, 0.0)
    earlier = (lax.broadcasted_iota(jnp.int32, (tt, tt), 1)
               < lax.broadcasted_iota(jnp.int32, (tt, tt), 0))
    prefix = jnp.dot(jnp.where(earlier, 1.0, 0.0).astype(BF16), onehot.astype(BF16),
                     preferred_element_type=F32)
    before = cnt_ref[...] + prefix
    rank1 = jnp.sum(jnp.where(lane == i1, before, 0.0), axis=-1, keepdims=True)
    rank2 = jnp.sum(jnp.where(lane == i2, before, 0.0), axis=-1, keepdims=True)
    cnt_ref[...] += jnp.sum(onehot, axis=0, keepdims=True)
    eid_ref[...] = jnp.where(lane == 0, i1 - n_groups,
                             jnp.where(lane == 1, i2 - n_groups,
                                       jnp.where(lane == 2, rank1,
                                                 jnp.where(lane == 3, rank2, 0.0)))).astype(jnp.int32)


def _norm2_router(x, g_all, layer, mod_l, row_map, w_rt_hi, w_rt_lo, b_rt, n_groups, per_group):
    n, d = x.shape
    tt = TOKEN_TILE
    kern = functools.partial(_norm2_router_kernel, n_groups=n_groups, per_group=per_group)
    return pl.pallas_call(
        kern,
        grid=(n // tt,),
        in_specs=[
            pl.BlockSpec((tt, d), lambda i: (i, 0)),
            pl.BlockSpec((None, 1, d), lambda i: (layer, 0, 0)),
            pl.BlockSpec((None, 6, d), lambda i: (row_map(i), 0, 0)),
            pl.BlockSpec((d, LANES), lambda i: (0, 0)),
            pl.BlockSpec((d, LANES), lambda i: (0, 0)),
            pl.BlockSpec((1, LANES), lambda i: (0, 0)),
        ],
        out_specs=[
            pl.BlockSpec((tt, d // 2), lambda i: (i, 0)),
            pl.BlockSpec((tt, LANES), lambda i: (i, 0)),
            pl.BlockSpec((tt, LANES), lambda i: (i, 0)),
            pl.BlockSpec((1, LANES), lambda i: (0, 0)),
        ],
        out_shape=[
            jax.ShapeDtypeStruct((n, d // 2), jnp.uint32),
            jax.ShapeDtypeStruct((n, LANES), jnp.int32),
            jax.ShapeDtypeStruct((n, LANES), F32),
            jax.ShapeDtypeStruct((1, LANES), F32),
        ],
        compiler_params=_cparams(("arbitrary",)),
        name="norm2_router",
    )(x, g_all, mod_l, w_rt_hi, w_rt_lo, b_rt)


def _final_norm_kernel(x_ref, g_ref, o_ref):
    x = x_ref[...]
    ms = jnp.mean(x * x, axis=-1, keepdims=True)
    o_ref[...] = x * lax.rsqrt(ms + EPS) * g_ref[...]


def _final_norm(x, g):
    n, d = x.shape
    tt = TOKEN_TILE
    return pl.pallas_call(
        _final_norm_kernel,
        grid=(n // tt,),
        in_specs=[pl.BlockSpec((tt, d), lambda i: (i, 0)),
                  pl.BlockSpec((1, d), lambda i: (0, 0))],
        out_specs=pl.BlockSpec((tt, d), lambda i: (i, 0)),
        out_shape=jax.ShapeDtypeStruct((n, d), F32),
        compiler_params=_cparams(("arbitrary",)),
        name="final_norm",
    )(x, g.reshape(1, d))


def _mm_kernel(x_ref, w_ref, o_ref, wbf_ref):
    @pl.when(pl.program_id(1) == 0)
    def _():
        _cast_rows(w_ref, wbf_ref, 512)

    o_ref[...] = jnp.dot(x_ref[...], wbf_ref[...], preferred_element_type=F32).astype(o_ref.dtype)


def _in_proj(hn, w_in_all, layer):
    m, k = hn.shape
    n = w_in_all.shape[2]
    tn = _largest_tile(n, (1024, 512, 256))
    tm = _largest_tile(m, (512, 256))
    return pl.pallas_call(
        _mm_kernel,
        grid=(n // tn, m // tm),
        in_specs=[
            pl.BlockSpec((tm, k), lambda j, i: (i, 0)),
            pl.BlockSpec((None, k, tn), lambda j, i: (layer, 0, j), pipeline_mode=pl.Buffered(1)),
        ],
        out_specs=pl.BlockSpec((tm, tn), lambda j, i: (i, j)),
        out_shape=jax.ShapeDtypeStruct((m, n), F32),
        scratch_shapes=[pltpu.VMEM((k, tn), BF16)],
        compiler_params=_cparams(("arbitrary", "arbitrary")),
        name="in_proj",
    )(hn, w_in_all)


MERGE_W_CHUNK = 256
MERGE_K_SPLIT = 2


def _merge_kernel(oap_ref, obp_ref, ocp_ref, oas_ref, obs_ref, ocs_ref, ga_ref, gb_ref, gc_ref, w_hbm,
                  o_ref, wbf, stage, sem, acc, *, layer, n_ctx_tiles):
    dk = oap_ref.shape[1]
    d = wbf.shape[0] // 3
    tn = o_ref.shape[1]
    ck = stage.shape[1]
    n_ck = wbf.shape[0] // ck
    kk = pl.program_id(2)

    @pl.when((pl.program_id(1) == 0) & (kk == 0))
    def _():
        col0 = pl.multiple_of(pl.program_id(0) * tn, tn)

        def copy(c, slot):
            r0 = pl.multiple_of(c * ck, ck)
            return pltpu.make_async_copy(w_hbm.at[layer, pl.ds(r0, ck), pl.ds(col0, tn)],
                                         stage.at[slot], sem.at[slot])

        copy(0, 0).start()

        def body(c, carry):
            slot = c & 1

            @pl.when(c + 1 < n_ck)
            def _():
                copy(c + 1, 1 - slot).start()

            copy(c, slot).wait()
            r0 = pl.multiple_of(c * ck, ck)
            wbf[pl.ds(r0, ck), :] = stage[slot].astype(BF16)
            return carry

        lax.fori_loop(0, n_ck, body, 0)

    def part(o_ref_, r):
        k0 = pl.multiple_of(r * d + kk * dk, dk)
        return jnp.dot(o_ref_[...], wbf[pl.ds(k0, dk), :], preferred_element_type=F32)

    def gate(g_ref_):
        return 0.5 * jnp.tanh(0.5 * g_ref_[...]) + 0.5

    def merge(oa_ref, ob_ref, oc_ref):
        branches = ((oa_ref, ga_ref), (ob_ref, gb_ref), (oc_ref, gc_ref))

        @pl.when(kk == 0)
        def _():
            for r, (x_ref, _) in enumerate(branches):
                acc[r] = part(x_ref, r)

        @pl.when((kk > 0) & (kk < MERGE_K_SPLIT - 1))
        def _():
            for r, (x_ref, _) in enumerate(branches):
                acc[r] += part(x_ref, r)

        @pl.when(kk == MERGE_K_SPLIT - 1)
        def _():
            out = None
            for r, (x_ref, g_ref_) in enumerate(branches):
                term = gate(g_ref_) * (acc[r] + part(x_ref, r))
                out = term if out is None else out + term
            o_ref[...] = out.astype(o_ref.dtype)

    @pl.when(pl.program_id(1) < n_ctx_tiles)
    def _():
        merge(oap_ref, obp_ref, ocp_ref)

    @pl.when(pl.program_id(1) >= n_ctx_tiles)
    def _():
        merge(oas_ref, obs_ref, ocs_ref)


def _merge(mix_ctx, mix_lat, proj, gate_off, w_branch_all, layer):
    n_p, d = mix_ctx[0].shape
    m = n_p + mix_lat[0].shape[0]
    n = w_branch_all.shape[2]
    tn = _largest_tile(math.gcd(n, gate_off), (1024, 512, 256))
    tm = TOKEN_TILE
    ck = MERGE_W_CHUNK
    nk = MERGE_K_SPLIT
    dk = d // nk
    assert nk >= 2 and d % nk == 0 and dk % ck == 0 and n_p % tm == 0
    gb = gate_off // tn
    nb = n // tn
    npt = n_p // tm
    cspec = pl.BlockSpec((tm, dk), lambda j, i, k: (jnp.minimum(i, npt - 1),
                                                    jnp.where(i < npt, k, nk - 1)))
    lspec = pl.BlockSpec((tm, dk), lambda j, i, k: (jnp.maximum(i - npt, 0),
                                                    jnp.where(i >= npt, k, 0)))
    gspec = lambda r: pl.BlockSpec((tm, tn), lambda j, i, k: (i, gb + r * nb + j))
    return pl.pallas_call(
        functools.partial(_merge_kernel, layer=layer, n_ctx_tiles=npt),
        grid=(nb, m // tm, nk),
        in_specs=[cspec, cspec, cspec, lspec, lspec, lspec, gspec(0), gspec(1), gspec(2),
                  pl.BlockSpec(memory_space=pl.ANY)],
        out_specs=pl.BlockSpec((tm, tn), lambda j, i, k: (i, j)),
        out_shape=jax.ShapeDtypeStruct((m, n), BF16),
        scratch_shapes=[pltpu.VMEM((3 * d, tn), BF16), pltpu.VMEM((2, ck, tn), F32),
                        pltpu.SemaphoreType.DMA((2,)), pltpu.VMEM((3, tm, tn), F32)],
        compiler_params=_cparams(("arbitrary", "arbitrary", "arbitrary")),
        name="branch_merge",
    )(*mix_ctx, *mix_lat, proj, proj, proj, w_branch_all)


def _out_proj_kernel(h_ref, w_ref, x_ref, mod_ref, o_ref, wbf_ref):
    @pl.when(pl.program_id(1) == 0)
    def _():
        _cast_rows(w_ref, wbf_ref, 512)

    mix = jnp.dot(h_ref[...], wbf_ref[...], preferred_element_type=F32)
    o_ref[...] = x_ref[...] + mod_ref[2:3, :] * mix


def _out_proj(merged, w_out_all, layer, x, mod_l, row_map):
    m, k = merged.shape
    n = w_out_all.shape[2]
    tn = _largest_tile(n, (1024, 512, 256))
    tm = TOKEN_TILE
    return pl.pallas_call(
        _out_proj_kernel,
        grid=(n // tn, m // tm),
        in_specs=[
            pl.BlockSpec((tm, k), lambda j, i: (i, 0)),
            pl.BlockSpec((None, k, tn), lambda j, i: (layer, 0, j), pipeline_mode=pl.Buffered(1)),
            pl.BlockSpec((tm, tn), lambda j, i: (i, j)),
            pl.BlockSpec((None, 6, tn), lambda j, i: (row_map(i), 0, j)),
        ],
        out_specs=pl.BlockSpec((tm, tn), lambda j, i: (i, j)),
        out_shape=jax.ShapeDtypeStruct((m, n), F32),
        scratch_shapes=[pltpu.VMEM((k, tn), BF16)],
        compiler_params=_cparams(("arbitrary", "arbitrary")),
        name="out_proj",
    )(merged, w_out_all, x, mod_l)


RG_CHUNK = 128


def _rglru_kernel(rx_ref, rg_ref, cw_ref, cb_ref, waf_ref, wxf_ref, wab_ref, wxb_ref,
                  ba_ref, bx_ref, lam_ref, h0_ref, o_ref, st_ref,
                  af, uf, ab, ub, wbf, *, seq):
    tc = RG_CHUNK
    n_chunks = seq // tc
    c = rx_ref.shape[1]
    wbf[0] = waf_ref[...].astype(BF16)
    wbf[1] = wxf_ref[...].astype(BF16)
    wbf[2] = wab_ref[...].astype(BF16)
    wbf[3] = wxb_ref[...].astype(BF16)
    cw = cw_ref[...]
    cb = cb_ref[...]

    def softplus(z):
        return jnp.maximum(z, 0.0) + jnp.log1p(jnp.exp(-jnp.abs(z)))

    sp_f = softplus(-lam_ref[0:1, :])
    sp_b = softplus(-lam_ref[1:2, :])
    row8 = lax.broadcasted_iota(jnp.int32, (tc, c), 0) & (SUBLANES - 1)

    def prep(k, carry):
        r0 = pl.multiple_of(k * tc, tc)
        cur = rx_ref[pl.ds(r0, tc), :]
        p0 = pl.multiple_of(jnp.maximum(r0 - SUBLANES, 0), SUBLANES)
        n0 = pl.multiple_of(jnp.minimum(r0 + tc, seq - SUBLANES), SUBLANES)
        prev8 = jnp.where(k > 0, rx_ref[pl.ds(p0, SUBLANES), :], 0.0)
        next8 = jnp.where(k < n_chunks - 1, rx_ref[pl.ds(n0, SUBLANES), :], 0.0)
        ext = jnp.concatenate([prev8, cur, next8], axis=0)
        ne = tc + 2 * SUBLANES
        xm2 = pltpu.roll(ext, 2, 0)[SUBLANES:SUBLANES + tc]
        xm1 = pltpu.roll(ext, 1, 0)[SUBLANES:SUBLANES + tc]
        xp1 = pltpu.roll(ext, ne - 1, 0)[SUBLANES:SUBLANES + tc]
        xc = cb + xm2 * cw[0:1] + xm1 * cw[1:2] + cur * cw[2:3] + xp1 * cw[3:4]
        xcb = xc.astype(BF16)

        def gates(wi, d, sp):
            r = _sigmoid(jnp.dot(xcb, wbf[wi], preferred_element_type=F32) + ba_ref[d:d + 1, :])
            i = _sigmoid(jnp.dot(xcb, wbf[wi + 1], preferred_element_type=F32) + bx_ref[d:d + 1, :])
            log_a = (-RG_C) * r * sp
            a = jnp.exp(log_a)
            u = jnp.sqrt(1.0 - a * a) * i * xc
            return a, u

        a, u = gates(0, 0, sp_f)
        for dd in (1, 2, 4):
            msk = row8 >= dd
            a_s = jnp.where(msk, _roll8(a, dd), 1.0)
            u_s = jnp.where(msk, _roll8(u, dd), 0.0)
            u = a * u_s + u
            a = a * a_s
        af[pl.ds(r0, tc), :] = a
        uf[pl.ds(r0, tc), :] = u

        a, u = gates(2, 1, sp_b)
        for dd in (1, 2, 4):
            msk = row8 < SUBLANES - dd
            a_s = jnp.where(msk, _roll8(a, SUBLANES - dd), 1.0)
            u_s = jnp.where(msk, _roll8(u, SUBLANES - dd), 0.0)
            u = a * u_s + u
            a = a * a_s
        ab[pl.ds(r0, tc), :] = a
        ub[pl.ds(r0, tc), :] = u
        return carry

    lax.fori_loop(0, n_chunks, prep, 0)

    n8 = seq // SUBLANES

    def carry_step(k, carry):
        hf, hb = carry
        i0 = pl.multiple_of(k * SUBLANES, SUBLANES)
        hf_t = af[pl.ds(i0, SUBLANES), :] * hf + uf[pl.ds(i0, SUBLANES), :]
        uf[pl.ds(i0, SUBLANES), :] = hf_t
        j0 = pl.multiple_of((n8 - 1 - k) * SUBLANES, SUBLANES)
        hb_t = ab[pl.ds(j0, SUBLANES), :] * hb + ub[pl.ds(j0, SUBLANES), :]
        ub[pl.ds(j0, SUBLANES), :] = hb_t
        return hf_t[SUBLANES - 1:SUBLANES, :], hb_t[0:1, :]

    hf, hb = lax.fori_loop(0, n8, carry_step, (h0_ref[0:1, :], h0_ref[1:2, :]))
    st_ref[0:1, :] = hf
    st_ref[1:2, :] = hb

    def finish(k, carry):
        r0 = pl.multiple_of(k * tc, tc)
        hsum = uf[pl.ds(r0, tc), :] + ub[pl.ds(r0, tc), :]
        o_ref[pl.ds(r0, tc), :] = (hsum * _gelu_tanh(rg_ref[pl.ds(r0, tc), :])).astype(o_ref.dtype)
        return carry

    lax.fori_loop(0, n_chunks, finish, 0)


def _rglru(proj, row_blk0, n_seq, seq, d, conv_w, conv_b, rg_wa, rg_wx, rg_ba, rg_bx, rg_lambda,
           h0, layer):
    nb, c = rg_wa.shape[2], rg_wa.shape[3]
    kern = functools.partial(_rglru_kernel, seq=seq)
    wspec = lambda direction: pl.BlockSpec((None, None, None, c, c),
                                           lambda b, n: (layer, direction, n, 0, 0))
    vec2 = pl.BlockSpec((None, 2, c), lambda b, n: (layer, 0, n))
    return pl.pallas_call(
        kern,
        grid=(n_seq, nb),
        in_specs=[
            pl.BlockSpec((seq, c), lambda b, n: (row_blk0 + b, n)),
            pl.BlockSpec((seq, c), lambda b, n: (row_blk0 + b, nb + n)),
            pl.BlockSpec((None, conv_w.shape[1], c), lambda b, n: (layer, 0, n)),
            pl.BlockSpec((None, 1, c), lambda b, n: (layer, 0, n)),
            wspec(0), wspec(0), wspec(1), wspec(1),
            vec2, vec2, vec2,
            pl.BlockSpec((None, 2, c), lambda b, n: (b, 0, n)),
        ],
        out_specs=[
            pl.BlockSpec((seq, c), lambda b, n: (b, n)),
            pl.BlockSpec((None, 2, c), lambda b, n: (b, 0, n)),
        ],
        out_shape=[
            jax.ShapeDtypeStruct((n_seq * seq, d), BF16),
            jax.ShapeDtypeStruct((n_seq, 2, d), F32),
        ],
        scratch_shapes=[pltpu.VMEM((seq, c), F32)] * 4 + [pltpu.VMEM((4, c, c), BF16)],
        compiler_params=_cparams(("arbitrary", "arbitrary")),
        name="rglru",
    )(proj, proj, conv_w, conv_b, rg_wa, rg_wx, rg_wa, rg_wx, rg_ba, rg_bx, rg_lambda, h0)


def _rope(x, cos_f, sin_f):
    return x * cos_f + pltpu.roll(x, HEAD_DIM // 2, 1) * sin_f


def _sink_softmax_pv(s, sink_col, v):
    m = jnp.maximum(jnp.max(s, axis=-1, keepdims=True), sink_col)
    p = jnp.exp(s - m)
    denom = jnp.sum(p, axis=-1, keepdims=True) + jnp.exp(sink_col - m)
    o = jnp.dot(p.astype(BF16), v, preferred_element_type=F32)
    return o / denom


def _sink_col(sink_ref, layer, hkv, group, rows_per_head):
    rows = group * rows_per_head
    assert rows_per_head & (rows_per_head - 1) == 0
    head = lax.broadcasted_iota(jnp.int32, (rows, 1), 0) >> (rows_per_head.bit_length() - 1)
    col = jnp.zeros((rows, 1), F32)
    for g in range(group):
        col = jnp.where(head == g, sink_ref[layer, hkv * group + g], col)
    return col


WIN_KV_PER_STEP = 2


def _kv_per_step(n_kv, *col_offsets, unit):
    kvs = WIN_KV_PER_STEP if n_kv % WIN_KV_PER_STEP == 0 else 1
    if any(off % (kvs * unit) for off in col_offsets):
        kvs = 1
    return kvs


def _win_ctx_kernel(sink_ref, q_ref, k_ref, v_ref, o_ref, *, layer, group, kvs):
    tq = q_ref.shape[0]
    hd = HEAD_DIM
    gw = group * hd
    for hh in range(kvs):
        hkv = pl.program_id(1) * kvs + hh
        q = jnp.concatenate([q_ref[:, hh * gw + g * hd:hh * gw + (g + 1) * hd] * ATTN_SCALE
                             for g in range(group)], axis=0)
        s = lax.dot_general(q.astype(BF16), k_ref[:, hh * hd:(hh + 1) * hd].astype(BF16), _DN_QKT,
                            preferred_element_type=F32)
        o = _sink_softmax_pv(s, _sink_col(sink_ref, layer, hkv, group, tq),
                             v_ref[:, hh * hd:(hh + 1) * hd].astype(BF16))
        for g in range(group):
            o_ref[:, hh * gw + g * hd:hh * gw + (g + 1) * hd] = o[g * tq:(g + 1) * tq].astype(o_ref.dtype)


def _win_ctx(proj, n_seq, seq, q_off, k_off, v_off, n_kv, group, win_sink, layer):
    gw = group * HEAD_DIM
    kvs = _kv_per_step(n_kv, k_off, v_off, unit=HEAD_DIM)
    kw = kvs * HEAD_DIM
    assert q_off % (kvs * gw) == 0
    kern = functools.partial(_win_ctx_kernel, layer=layer, group=group, kvs=kvs)
    return pl.pallas_call(
        kern,
        grid_spec=pltpu.PrefetchScalarGridSpec(
            num_scalar_prefetch=1,
            grid=(n_seq, n_kv // kvs),
            in_specs=[
                pl.BlockSpec((seq, kvs * gw), lambda b, h, s: (b, q_off // (kvs * gw) + h)),
                pl.BlockSpec((seq, kw), lambda b, h, s: (b, k_off // kw + h)),
                pl.BlockSpec((seq, kw), lambda b, h, s: (b, v_off // kw + h)),
            ],
            out_specs=pl.BlockSpec((seq, kvs * gw), lambda b, h, s: (b, h)),
        ),
        out_shape=jax.ShapeDtypeStruct((n_seq * seq, n_kv * gw), BF16),
        compiler_params=_cparams(("arbitrary", "arbitrary")),
        name="win_attn_ctx",
    )(win_sink, proj, proj, proj)


def _win_lat_kernel(sink_ref, q_ref, kp_ref, kc_ref, kn_ref, vp_ref, vc_ref, vn_ref, ck_ref, cv_ref,
                    cq_ref, sq_ref, cp_ref, sp_ref, cc_ref, sc_ref, cn_ref, sn_ref, o_ref, *,
                    layer, group, n_qblk, kvs):
    qi = pl.program_id(2)
    tq = q_ref.shape[0]
    past = ck_ref.shape[0]
    hd = HEAD_DIM
    gw = group * hd
    cq, sq = cq_ref[...], sq_ref[...]
    cp, sp, cc, sc, cn, sn = (cp_ref[...], sp_ref[...], cc_ref[...], sc_ref[...], cn_ref[...],
                              sn_ref[...])
    shape = (group * tq, past + 3 * tq)
    r = lax.broadcasted_iota(jnp.int32, shape, 0) & (tq - 1)
    c = lax.broadcasted_iota(jnp.int32, shape, 1) - past
    in_prev = (c >= 0) & (c < tq)
    in_next = c >= 2 * tq
    bad = (in_prev & ((c < r + (tq - WINDOW)) | (qi == 0))) | \
          (in_next & ((c - 2 * tq > r - (tq - WINDOW)) | (qi == n_qblk - 1)))
    for hh in range(kvs):
        hkv = pl.program_id(1) * kvs + hh
        kcol = slice(hh * hd, (hh + 1) * hd)
        q = jnp.concatenate(
            [_rope(q_ref[:, hh * gw + g * hd:hh * gw + (g + 1) * hd], cq, sq) * ATTN_SCALE
             for g in range(group)], axis=0).astype(BF16)
        k = jnp.concatenate([
            ck_ref[:, kcol].astype(BF16),
            _rope(kp_ref[:, kcol], cp, sp).astype(BF16),
            _rope(kc_ref[:, kcol], cc, sc).astype(BF16),
            _rope(kn_ref[:, kcol], cn, sn).astype(BF16)], axis=0)
        v = jnp.concatenate([cv_ref[:, kcol].astype(BF16), vp_ref[:, kcol].astype(BF16),
                             vc_ref[:, kcol].astype(BF16), vn_ref[:, kcol].astype(BF16)], axis=0)
        s = lax.dot_general(q, k, _DN_QKT, preferred_element_type=F32)
        s = jnp.where(bad, NEG_INF, s)
        o = _sink_softmax_pv(s, _sink_col(sink_ref, layer, hkv, group, tq), v)
        for g in range(group):
            o_ref[:, hh * gw + g * hd:hh * gw + (g + 1) * hd] = o[g * tq:(g + 1) * tq].astype(o_ref.dtype)


def _win_lat(proj, row0, n_seq, seq, q_off, k_off, v_off, n_kv, group, win_sink, layer,
             cache_k, cache_v, cos_f, sin_f):
    tq = WINDOW
    assert seq % tq == 0 and row0 % tq == 0
    nq = seq // tq
    gw = group * HEAD_DIM
    kvs = _kv_per_step(n_kv, k_off, v_off, unit=HEAD_DIM)
    kw = kvs * HEAD_DIM
    assert q_off % (kvs * gw) == 0
    past = cache_k.shape[2]
    rb0 = row0 // tq
    kb, vb = k_off // kw, v_off // kw
    kern = functools.partial(_win_lat_kernel, layer=layer, group=group, n_qblk=nq, kvs=kvs)
    prv = lambda i: jnp.maximum(i - 1, 0)
    nxt = lambda i: jnp.minimum(i + 1, nq - 1)
    kv = lambda colb, f: pl.BlockSpec((tq, kw), lambda b, h, i, s: (rb0 + b * nq + f(i), colb + h))
    cache = pl.BlockSpec((None, None, past, kw), lambda b, h, i, s: (b, layer, 0, h))
    tab = lambda f: pl.BlockSpec((tq, HEAD_DIM), lambda b, h, i, s: (f(i), 0))
    same = lambda i: i
    return pl.pallas_call(
        kern,
        grid_spec=pltpu.PrefetchScalarGridSpec(
            num_scalar_prefetch=1,
            grid=(n_seq, n_kv // kvs, nq),
            in_specs=[
                pl.BlockSpec((tq, kvs * gw),
                             lambda b, h, i, s: (rb0 + b * nq + i, q_off // (kvs * gw) + h)),
                kv(kb, prv), kv(kb, same), kv(kb, nxt),
                kv(vb, prv), kv(vb, same), kv(vb, nxt),
                cache, cache,
                tab(same), tab(same), tab(prv), tab(prv), tab(same), tab(same), tab(nxt), tab(nxt),
            ],
            out_specs=pl.BlockSpec((tq, kvs * gw), lambda b, h, i, s: (b * nq + i, h)),
        ),
        out_shape=jax.ShapeDtypeStruct((n_seq * seq, n_kv * gw), BF16),
        compiler_params=_cparams(("arbitrary", "arbitrary", "arbitrary")),
        name="win_attn_lat",
    )(win_sink, proj, proj, proj, proj, proj, proj, proj, cache_k, cache_v,
      cos_f, sin_f, cos_f, sin_f, cos_f, sin_f, cos_f, sin_f)


def _diff_lambda(l4, lam_init):
    t1 = jnp.sum(l4[0:1, :] * l4[1:2, :], axis=-1, keepdims=True)
    t2 = jnp.sum(l4[2:3, :] * l4[3:4, :], axis=-1, keepdims=True)
    return jnp.exp(t1) - jnp.exp(t2) + lam_init


ATTN_SCALE = HEAD_DIM ** -0.5
_DN_QKT = (((1,), (1,)), ((), ()))


def _subln(o, g, lam_init):
    ms = jnp.mean(o * o, axis=-1, keepdims=True)
    return o * lax.rsqrt(ms + EPS) * g * (1.0 - lam_init)


def _diff_core(q0, q1, k0, k1, v, lam, g, lam_init):
    s0 = lax.dot_general(q0, k0, _DN_QKT, preferred_element_type=F32)
    s1 = lax.dot_general(q1, k1, _DN_QKT, preferred_element_type=F32)
    e0 = jnp.exp(s0 - jnp.max(s0, axis=-1, keepdims=True))
    e1 = jnp.exp(s1 - jnp.max(s1, axis=-1, keepdims=True))
    c0 = 1.0 / jnp.sum(e0, axis=-1, keepdims=True)
    c1 = lam / jnp.sum(e1, axis=-1, keepdims=True)
    a = e0 * c0 - e1 * c1
    o = jnp.dot(a.astype(BF16), v, preferred_element_type=F32)
    return _subln(o, g, lam_init)


def _diff_ctx_kernel(q_ref, k_ref, v_ref, l4_ref, g_ref, o_ref, *, lam_init):
    lam = _diff_lambda(l4_ref[...], lam_init)
    hd = HEAD_DIM
    for h in range(q_ref.shape[1] // (2 * hd)):
        c0, c1, c2 = 2 * h * hd, (2 * h + 1) * hd, (2 * h + 2) * hd
        o = _diff_core((q_ref[:, c0:c1] * ATTN_SCALE).astype(BF16),
                       (q_ref[:, c1:c2] * ATTN_SCALE).astype(BF16),
                       k_ref[:, c0:c1].astype(BF16), k_ref[:, c1:c2].astype(BF16),
                       v_ref[:, c0:c2].astype(BF16), lam, g_ref[...], lam_init)
        o_ref[:, c0:c2] = o.astype(o_ref.dtype)


DIFF_CTX_HEADS_PER_STEP = 2


def _diff_ctx(proj, n_seq, seq, q_off, k_off, v_off, n_heads, diff_lambda, subln_g, layer, lam_init):
    hps = DIFF_CTX_HEADS_PER_STEP if n_heads % DIFF_CTX_HEADS_PER_STEP == 0 else 1
    hw = 2 * HEAD_DIM
    bw = hps * hw
    kern = functools.partial(_diff_ctx_kernel, lam_init=lam_init)
    blk = lambda off: pl.BlockSpec((seq, bw), lambda b, h: (b, off // bw + h))
    assert q_off % bw == 0 and k_off % bw == 0 and v_off % bw == 0
    return pl.pallas_call(
        kern,
        grid=(n_seq, n_heads // hps),
        in_specs=[blk(q_off), blk(k_off), blk(v_off),
                  pl.BlockSpec((None, 4, HEAD_DIM), lambda b, h: (layer, 0, 0)),
                  pl.BlockSpec((None, 1, hw), lambda b, h: (layer, 0, 0))],
        out_specs=pl.BlockSpec((seq, bw), lambda b, h: (b, h)),
        out_shape=jax.ShapeDtypeStruct((n_seq * seq, n_heads * hw), BF16),
        compiler_params=_cparams(("arbitrary", "arbitrary")),
        name="diff_attn_ctx",
    )(proj, proj, proj, diff_lambda, subln_g)


DIFF_KV_CHUNK = 256


def _diff_lat_kernel(q_ref, k_ref, v_ref, ck_ref, cv_ref, cq_ref, sq_ref, call_ref, sall_ref,
                     l4_ref, g_ref, o_ref, kall, vall, *, lam_init):
    hd = HEAD_DIM
    past = ck_ref.shape[0]
    seq = k_ref.shape[0]

    @pl.when(pl.program_id(2) == 0)
    def _():
        kall[0:past, :] = ck_ref[...].astype(BF16)
        vall[0:past, :] = cv_ref[...].astype(BF16)

        def fill(i, carry):
            r0 = pl.multiple_of(i * DIFF_KV_CHUNK, DIFF_KV_CHUNK)
            rows = pl.ds(r0, DIFF_KV_CHUNK)
            dst = pl.ds(past + r0, DIFF_KV_CHUNK)
            cs, sn = call_ref[rows, :], sall_ref[rows, :]
            kall[dst, 0:hd] = _rope(k_ref[rows, 0:hd], cs, sn).astype(BF16)
            kall[dst, hd:2 * hd] = _rope(k_ref[rows, hd:2 * hd], cs, sn).astype(BF16)
            vall[dst, :] = v_ref[rows, :].astype(BF16)
            return carry

        lax.fori_loop(0, seq // DIFF_KV_CHUNK, fill, 0)

    lam = _diff_lambda(l4_ref[...], lam_init)
    cq, sq = cq_ref[...], sq_ref[...]
    q0 = (_rope(q_ref[:, 0:hd], cq, sq) * ATTN_SCALE).astype(BF16)
    q1 = (_rope(q_ref[:, hd:2 * hd], cq, sq) * ATTN_SCALE).astype(BF16)
    o = _diff_core(q0, q1, kall[:, 0:hd], kall[:, hd:2 * hd], vall[...], lam, g_ref[...], lam_init)
    o_ref[...] = o.astype(o_ref.dtype)


def _diff_lat(proj, row0, n_seq, seq, q_off, k_off, v_off, n_heads, diff_lambda, subln_g, layer,
              lam_init, cache_k, cache_v, cos_f, sin_f):
    hw = 2 * HEAD_DIM
    tq = 256
    kc = DIFF_KV_CHUNK
    assert seq % kc == 0 and row0 % seq == 0 and seq % tq == 0
    nq = seq // tq
    past = cache_k.shape[2]
    sb0 = row0 // seq
    kern = functools.partial(_diff_lat_kernel, lam_init=lam_init)
    full = lambda off: pl.BlockSpec((seq, hw), lambda b, h, i: (sb0 + b, off // hw + h))
    cache = pl.BlockSpec((None, None, past, hw), lambda b, h, i: (b, layer, 0, h))
    return pl.pallas_call(
        kern,
        grid=(n_seq, n_heads, nq),
        in_specs=[
            pl.BlockSpec((tq, hw), lambda b, h, i: (sb0 * nq + b * nq + i, q_off // hw + h)),
            full(k_off), full(v_off), cache, cache,
            pl.BlockSpec((tq, HEAD_DIM), lambda b, h, i: (i, 0)),
            pl.BlockSpec((tq, HEAD_DIM), lambda b, h, i: (i, 0)),
            pl.BlockSpec((seq, HEAD_DIM), lambda b, h, i: (0, 0)),
            pl.BlockSpec((seq, HEAD_DIM), lambda b, h, i: (0, 0)),
            pl.BlockSpec((None, 4, HEAD_DIM), lambda b, h, i: (layer, 0, 0)),
            pl.BlockSpec((None, 1, hw), lambda b, h, i: (layer, 0, 0)),
        ],
        out_specs=pl.BlockSpec((tq, hw), lambda b, h, i: (b * nq + i, h)),
        out_shape=jax.ShapeDtypeStruct((n_seq * seq, n_heads * hw), BF16),
        scratch_shapes=[pltpu.VMEM((past + seq, hw), BF16), pltpu.VMEM((past + seq, hw), BF16)],
        compiler_params=_cparams(("arbitrary", "arbitrary", "arbitrary")),
        name="diff_attn_lat",
    )(proj, proj, proj, cache_k, cache_v, cos_f, sin_f, cos_f, sin_f, diff_lambda, subln_g)


def _moe_plan(eid, rank, counts, n_experts, rows_per_blk):
    n = eid.shape[0]
    a = n * TOP_K
    flat_e = eid.reshape(-1)
    nblk_e = (counts + rows_per_blk - 1) // rows_per_blk
    blk_end = jnp.cumsum(nblk_e)
    blk_start = blk_end - nblk_e
    is_e = flat_e[:, None] == jnp.arange(n_experts, dtype=jnp.int32)[None, :]
    start_of = jnp.sum(jnp.where(is_e, blk_start[None, :], 0), axis=1)
    dest = (start_of * rows_per_blk + rank.reshape(-1)).astype(jnp.int32)
    n_blk = a // rows_per_blk + n_experts
    n_used = blk_end[-1].astype(jnp.int32)
    slot_tok = jnp.zeros((n_blk * rows_per_blk,), jnp.int32).at[dest].set(
        jnp.arange(a, dtype=jnp.int32) // TOP_K)
    blk_ids = jnp.arange(n_blk, dtype=jnp.int32)
    blk_e = jnp.minimum(jnp.searchsorted(blk_end, blk_ids, side='right'), n_experts - 1)
    blk_e = jnp.where(blk_ids < n_used, blk_e, blk_e[jnp.maximum(n_used - 1, 0)]).astype(jnp.int32)
    first = jnp.concatenate([jnp.ones((1,), jnp.int32),
                             (blk_e[1:] != blk_e[:-1]).astype(jnp.int32)])
    return slot_tok, dest, blk_e, first, n_used.reshape(1)


def _gather_rows_kernel(tok_ref, x_hbm, o_ref, buf, sem, *, rows_per_step):
    i = pl.program_id(0)
    slot = i & 1

    def issue(step, dst_slot):
        base = step * rows_per_step

        def body(r, carry):
            tok = tok_ref[base + r]
            pltpu.make_async_copy(x_hbm.at[pl.ds(tok, 1)], buf.at[dst_slot, pl.ds(r, 1)],
                                  sem.at[dst_slot]).start()
            return carry

        lax.fori_loop(0, rows_per_step, body, 0, unroll=DMA_ISSUE_UNROLL)

    @pl.when(i == 0)
    def _():
        issue(0, 0)

    @pl.when(i + 1 < pl.num_programs(0))
    def _():
        issue(i + 1, 1 - slot)

    pltpu.make_async_copy(x_hbm.at[pl.ds(0, rows_per_step)], buf.at[slot], sem.at[slot]).wait()
    w = buf.shape[2]
    lo, hi = _unpack_bf16_pairs(buf[slot])
    o_ref[:, 0:w] = lo.astype(o_ref.dtype)
    o_ref[:, w:2 * w] = hi.astype(o_ref.dtype)


def _gather_rows(x, slot_tok):
    rows = slot_tok.shape[0]
    w = x.shape[1]
    d = 2 * w
    rps = MOE_ROWS
    kern = functools.partial(_gather_rows_kernel, rows_per_step=rps)
    return pl.pallas_call(
        kern,
        grid_spec=pltpu.PrefetchScalarGridSpec(
            num_scalar_prefetch=1,
            grid=(rows // rps,),
            in_specs=[pl.BlockSpec(memory_space=pl.ANY)],
            out_specs=pl.BlockSpec((rps, d), lambda i, tok: (i, 0)),
            scratch_shapes=[pltpu.VMEM((2, rps, w), x.dtype), pltpu.SemaphoreType.DMA((2,))],
        ),
        out_shape=jax.ShapeDtypeStruct((rows, d), BF16),
        compiler_params=_cparams(("arbitrary",)),
        name="moe_gather",
    )(slot_tok, x)


def _gmm1_kernel(be_ref, first_ref, nused_ref, x_ref, wg_ref, wu_ref, o_ref, wgb, wub):
    blk = pl.program_id(1)

    @pl.when(first_ref[blk] == 1)
    def _():
        _cast_rows(wg_ref, wgb, 512)
        _cast_rows(wu_ref, wub, 512)

    @pl.when(blk < nused_ref[0])
    def _():
        x = x_ref[...]
        g = jnp.dot(x, wgb[...], preferred_element_type=F32)
        u = jnp.dot(x, wub[...], preferred_element_type=F32)
        o_ref[...] = (g * _sigmoid(g) * u).astype(o_ref.dtype)

    @pl.when(blk >= nused_ref[0])
    def _():
        o_ref[...] = jnp.zeros(o_ref.shape, o_ref.dtype)


def _gmm1(xs, w_gate_all, w_up_all, layer, blk_e, first, n_used):
    rows, d = xs.shape
    de = w_gate_all.shape[3]
    tn = _largest_tile(de, (512, 256, 128))
    r = MOE_ROWS
    wspec = pl.BlockSpec((None, None, d, tn), lambda j, b, be, fi, nu: (layer, be[b], 0, j))
    return pl.pallas_call(
        _gmm1_kernel,
        grid_spec=pltpu.PrefetchScalarGridSpec(
            num_scalar_prefetch=3,
            grid=(de // tn, rows // r),
            in_specs=[pl.BlockSpec((r, d), lambda j, b, be, fi, nu: (b, 0)), wspec, wspec],
            out_specs=pl.BlockSpec((r, tn), lambda j, b, be, fi, nu: (b, j)),
            scratch_shapes=[pltpu.VMEM((d, tn), BF16), pltpu.VMEM((d, tn), BF16)],
        ),
        out_shape=jax.ShapeDtypeStruct((rows, de), BF16),
        compiler_params=_cparams(("arbitrary", "arbitrary")),
        name="moe_gate_up",
    )(blk_e, first, n_used, xs, w_gate_all, w_up_all)


def _gmm2_kernel(be_ref, first_ref, nused_ref, h_ref, wd_ref, o_ref, wdb):
    blk = pl.program_id(1)

    @pl.when(first_ref[blk] == 1)
    def _():
        _cast_rows(wd_ref, wdb, 512)

    @pl.when(blk < nused_ref[0])
    def _():
        o_ref[...] = _pack_bf16_pairs(jnp.dot(h_ref[...], wdb[...], preferred_element_type=F32))

    @pl.when(blk >= nused_ref[0])
    def _():
        o_ref[...] = jnp.zeros(o_ref.shape, o_ref.dtype)


def _gmm2(h, w_down_all, layer, blk_e, first, n_used):
    rows, de = h.shape
    d = w_down_all.shape[3]
    tn = _largest_tile(d, (2048, 1024, 512, 256))
    r = MOE_ROWS
    ys = pl.pallas_call(
        _gmm2_kernel,
        grid_spec=pltpu.PrefetchScalarGridSpec(
            num_scalar_prefetch=3,
            grid=(d // tn, rows // r),
            in_specs=[
                pl.BlockSpec((r, de), lambda j, b, be, fi, nu: (b, 0)),
                pl.BlockSpec((None, None, de, tn), lambda j, b, be, fi, nu: (layer, be[b], 0, j)),
            ],
            out_specs=pl.BlockSpec((r, tn // 2), lambda j, b, be, fi, nu: (b, j)),
            scratch_shapes=[pltpu.VMEM((de, tn), BF16)],
        ),
        out_shape=jax.ShapeDtypeStruct((rows, d // 2), jnp.uint32),
        compiler_params=_cparams(("arbitrary", "arbitrary")),
        name="moe_down",
    )(blk_e, first, n_used, h, w_down_all)
    return ys, tn


def _combine_kernel(slot_ref, x_ref, wts_ref, mod_ref, ys_hbm, o_ref, buf, sem, *, pack_cols):
    tt = x_ref.shape[0]
    base = pl.program_id(0) * tt

    def issue(r, carry):
        for k in range(TOP_K):
            s = slot_ref[TOP_K * (base + r) + k]
            pltpu.make_async_copy(ys_hbm.at[pl.ds(s, 1)], buf.at[k, pl.ds(r, 1)], sem).start()
        return carry

    lax.fori_loop(0, tt, issue, 0, unroll=DMA_ISSUE_UNROLL)
    for k in range(TOP_K):
        pltpu.make_async_copy(ys_hbm.at[pl.ds(0, tt)], buf.at[k], sem).wait()
    w = wts_ref[...]
    w0, w1 = w[:, 0:1], w[:, 1:2]
    half = pack_cols // 2
    for j in range(x_ref.shape[1] // pack_cols):
        words = slice(j * half, (j + 1) * half)
        lo0, hi0 = _unpack_bf16_pairs(buf[0, :, words])
        lo1, hi1 = _unpack_bf16_pairs(buf[1, :, words])
        for y, c0 in ((w0 * lo0 + w1 * lo1, j * pack_cols), (w0 * hi0 + w1 * hi1, j * pack_cols + half)):
            cols = slice(c0, c0 + half)
            o_ref[:, cols] = x_ref[:, cols] + mod_ref[5:6, cols] * y


def _combine(x, wts, mod_l, row_map, ys, slots, pack_cols):
    n, d = x.shape
    tt = TOKEN_TILE
    return pl.pallas_call(
        functools.partial(_combine_kernel, pack_cols=pack_cols),
        grid_spec=pltpu.PrefetchScalarGridSpec(
            num_scalar_prefetch=1,
            grid=(n // tt,),
            in_specs=[
                pl.BlockSpec((tt, d), lambda i, s: (i, 0)),
                pl.BlockSpec((tt, LANES), lambda i, s: (i, 0)),
                pl.BlockSpec((None, 6, d), lambda i, s: (row_map(i), 0, 0)),
                pl.BlockSpec(memory_space=pl.ANY),
            ],
            out_specs=pl.BlockSpec((tt, d), lambda i, s: (i, 0)),
            scratch_shapes=[pltpu.VMEM((TOP_K, tt, d // 2), jnp.uint32), pltpu.SemaphoreType.DMA(())],
        ),
        out_shape=jax.ShapeDtypeStruct((n, d), F32),
        compiler_params=_cparams(("arbitrary",)),
        name="moe_combine",
    )(slots, x, wts, mod_l, ys)


def _rope_tables(seq):
    rows = seq // GRID_W
    row = jnp.repeat(jnp.arange(rows), GRID_W).astype(F32)
    col = jnp.tile(jnp.arange(GRID_W), rows).astype(F32)
    n_freq = HEAD_DIM // 4
    inv = 1.0 / (ROPE_BASE ** (jnp.arange(n_freq, dtype=F32) / n_freq))
    ang = jnp.concatenate([row[:, None] * inv, col[:, None] * inv], axis=-1)
    cos, sin = jnp.cos(ang), jnp.sin(ang)
    return jnp.concatenate([cos, cos], axis=-1), jnp.concatenate([-sin, sin], axis=-1)


def kernel(x_prompt, x_sample, c, cache_win_k, cache_win_v, cache_diff_k, cache_diff_v, state_rnn, c_ctx, w_mod, b_mod, norm1_g, norm2_g, final_g, w_in, conv_w, conv_b, rg_wa, rg_ba, rg_wx, rg_bx, rg_lambda, win_sink, diff_lambda, diff_subln_g, w_branch, w_out, moe_w_group, moe_b_group, moe_w_router, moe_b_router, moe_w_gate, moe_w_up, moe_w_down):
    bp, tp, d = x_prompt.shape
    bs, ts, _ = x_sample.shape
    depth = w_in.shape[0]
    past = cache_win_k.shape[2]
    n_kv = cache_win_k.shape[3]
    n_win = win_sink.shape[1]
    group = n_win // n_kv
    n_diff = cache_diff_k.shape[3]
    d_rnn = rg_ba.shape[2]
    n_groups = moe_w_group.shape[2]
    n_experts = moe_w_router.shape[2]
    per_group = n_experts // n_groups
    n_p, n_s = bp * tp, bs * ts
    tt = TOKEN_TILE
    assert tp % tt == 0 and ts % tt == 0 and n_p % ts == 0 and d_rnn == d
    assert n_groups + n_experts <= LANES

    sections = (d_rnn, d_rnn, n_win * HEAD_DIM, n_kv * HEAD_DIM, n_kv * HEAD_DIM,
                n_diff * 2 * HEAD_DIM, n_diff * 2 * HEAD_DIM, n_diff * 2 * HEAD_DIM, 3 * d)
    offs = [0]
    for s_ in sections:
        offs.append(offs[-1] + s_)
    _, _, o_wq, o_wk, o_wv, o_dq, o_dk, o_dv, o_mg = offs[:9]

    row_map = _mod_row_map(n_p // tt, ts // tt)

    n_rows = -(-(1 + bs) // SUBLANES) * SUBLANES
    cvecs = jnp.zeros((n_rows, d), F32).at[0].set(c_ctx).at[1:1 + bs].set(c)
    mod = _ada_mod_all(cvecs, w_mod, b_mod).reshape(depth, n_rows, 6, d)

    cos_f, sin_f = _rope_tables(ts)
    ck_win = cache_win_k.reshape(bs, depth, past, n_kv * HEAD_DIM)
    cv_win = cache_win_v.reshape(bs, depth, past, n_kv * HEAD_DIM)
    ck_diff = cache_diff_k.reshape(bs, depth, past, n_diff * 2 * HEAD_DIM)
    cv_diff = cache_diff_v.reshape(bs, depth, past, n_diff * 2 * HEAD_DIM)
    w_rt = jnp.zeros((depth, d, LANES), F32).at[:, :, :n_groups].set(moe_w_group) \
        .at[:, :, n_groups:n_groups + n_experts].set(moe_w_router)
    b_rt = jnp.zeros((depth, 1, LANES), F32).at[:, 0, :n_groups].set(moe_b_group) \
        .at[:, 0, n_groups:n_groups + n_experts].set(moe_b_router)
    w_rt_hi = w_rt.astype(BF16)
    w_rt_lo = (w_rt - w_rt_hi.astype(F32)).astype(BF16)
    h0_prompt = jnp.zeros((bp, 2, d_rnn), F32)
    norm1_g = norm1_g.reshape(depth, 1, d)
    norm2_g = norm2_g.reshape(depth, 1, d)
    conv_b = conv_b.reshape(depth, 1, d_rnn)
    diff_subln_g = diff_subln_g.reshape(depth, 1, 2 * HEAD_DIM)

    x = jnp.concatenate([x_prompt.reshape(n_p, d), x_sample.reshape(n_s, d)], axis=0)
    new_wk, new_wv, new_dk, new_dv, new_st = [], [], [], [], []
    for l in range(depth):
        lam_init = 0.8 - 0.6 * math.exp(-0.3 * l)
        mod_l = mod[l]
        hn = _norm1(x, norm1_g, l, mod_l, row_map)
        proj = _in_proj(hn, w_in, l)

        rg_args = (conv_w, conv_b, rg_wa, rg_wx, rg_ba, rg_bx, rg_lambda)
        oa_p, st_p = _rglru(proj, 0, bp, tp, d_rnn, *rg_args, h0_prompt, l)
        oa_s, _ = _rglru(proj, n_p // ts, bs, ts, d_rnn, *rg_args, state_rnn[:, l], l)

        ob_p = _win_ctx(proj, bp, tp, o_wq, o_wk, o_wv, n_kv, group, win_sink, l)
        ob_s = _win_lat(proj, n_p, bs, ts, o_wq, o_wk, o_wv, n_kv, group, win_sink, l,
                        ck_win, cv_win, cos_f, sin_f)
        oc_p = _diff_ctx(proj, bp, tp, o_dq, o_dk, o_dv, n_diff, diff_lambda, diff_subln_g, l, lam_init)
        oc_s = _diff_lat(proj, n_p, bs, ts, o_dq, o_dk, o_dv, n_diff, diff_lambda, diff_subln_g, l,
                         lam_init, ck_diff, cv_diff, cos_f, sin_f)

        merged = _merge((oa_p, ob_p, oc_p), (oa_s, ob_s, oc_s), proj, o_mg, w_branch, l)
        x = _out_proj(merged, w_out, l, x, mod_l, row_map)

        hn2, eid, wts, cnt = _norm2_router(x, norm2_g, l, mod_l, row_map, w_rt_hi[l], w_rt_lo[l],
                                           b_rt[l], n_groups, per_group)
        counts = cnt[0, n_groups:n_groups + n_experts].astype(jnp.int32)
        slot_tok, dest, blk_e, first, n_used = _moe_plan(
            eid[:, :TOP_K], eid[:, TOP_K:2 * TOP_K], counts, n_experts, MOE_ROWS)
        xs = _gather_rows(hn2, slot_tok)
        hmid = _gmm1(xs, moe_w_gate, moe_w_up, l, blk_e, first, n_used)
        ys, pack_cols = _gmm2(hmid, moe_w_down, l, blk_e, first, n_used)
        x = _combine(x, wts, mod_l, row_map, ys, dest, pack_cols)

        pr = proj[:n_p]
        new_wk.append(pr[:, o_wk:o_wv].reshape(bp, tp, n_kv, HEAD_DIM))
        new_wv.append(pr[:, o_wv:o_dq].reshape(bp, tp, n_kv, HEAD_DIM))
        new_dk.append(pr[:, o_dk:o_dv].reshape(bp, tp, n_diff, 2, HEAD_DIM))
        new_dv.append(pr[:, o_dv:o_mg].reshape(bp, tp, n_diff, 2 * HEAD_DIM))
        new_st.append(st_p)

    y = _final_norm(x, final_g)
    return (y[:n_p].reshape(bp, tp, d), y[n_p:].reshape(bs, ts, d),
            jnp.stack(new_wk, axis=1), jnp.stack(new_wv, axis=1),
            jnp.stack(new_dk, axis=1), jnp.stack(new_dv, axis=1), jnp.stack(new_st, axis=1))
```

```python
import functools
import math

import jax
import jax.numpy as jnp
from jax import lax
from jax.experimental import pallas as pl
from jax.experimental.pallas import tpu as pltpu

F32 = jnp.float32
BF16 = jnp.bfloat16

HEAD_DIM = 128
GRID_W = 64
ROPE_BASE = 10000.0
WINDOW = 128
CONV_LEFT = 2
RG_C = 8.0
TOP_K = 2
EPS = 1e-6
NEG_INF = -1e30

V7X_VMEM_LIMIT_BYTES = 56 * 1024 * 1024
LANES = 128
SUBLANES = 8

TOKEN_TILE = 256
MOE_ROWS = 256
DMA_ISSUE_UNROLL = 8


def _cparams(sem):
    return pltpu.CompilerParams(dimension_semantics=sem, vmem_limit_bytes=V7X_VMEM_LIMIT_BYTES)


def _sigmoid(x):
    return 0.5 * jnp.tanh(0.5 * x) + 0.5


def _roll8(x, shift):
    r, c = x.shape
    return pltpu.roll(x.reshape(r // SUBLANES, SUBLANES, c), shift, 1).reshape(r, c)


def _pack_bf16_pairs(x):
    w = x.shape[1] // 2
    lo = lax.bitcast_convert_type(x[:, :w].astype(BF16).astype(F32), jnp.uint32) >> 16
    hi = lax.bitcast_convert_type(x[:, w:].astype(BF16).astype(F32), jnp.uint32) & jnp.uint32(0xFFFF0000)
    return hi | lo


def _unpack_bf16_pairs(p):
    lo = lax.bitcast_convert_type(p << 16, F32)
    hi = lax.bitcast_convert_type(p & jnp.uint32(0xFFFF0000), F32)
    return lo, hi


def _gelu_tanh(x):
    c = math.sqrt(2.0 / math.pi)
    return 0.5 * x * (1.0 + jnp.tanh(c * (x + 0.044715 * (x * x * x))))


def _largest_tile(n, candidates):
    for c in candidates:
        if n % c == 0:
            return c
    raise ValueError(f"no tile in {candidates} divides {n}")


def _cast_rows(src_ref, dst_ref, chunk):
    rows = src_ref.shape[0]
    chunk = min(chunk, rows)

    def body(k, c):
        r0 = pl.multiple_of(k * chunk, chunk)
        dst_ref[pl.ds(r0, chunk), :] = src_ref[pl.ds(r0, chunk), :].astype(dst_ref.dtype)
        return c

    lax.fori_loop(0, rows // chunk, body, 0)


def _adamod_kernel(cv_ref, w_ref, b_ref, o_ref):
    cv = cv_ref[...]
    s = cv * _sigmoid(cv)
    o_ref[...] = jnp.dot(s.astype(BF16), w_ref[...].astype(BF16),
                         preferred_element_type=F32) + b_ref[...]


def _ada_mod_all(cvecs, w_mod, b_mod):
    depth, d, n6 = w_mod.shape
    rows = cvecs.shape[0]
    tn = _largest_tile(n6, (512, 256, 128))
    return pl.pallas_call(
        _adamod_kernel,
        grid=(depth, n6 // tn),
        in_specs=[
            pl.BlockSpec((rows, d), lambda l, j: (0, 0)),
            pl.BlockSpec((None, d, tn), lambda l, j: (l, 0, j)),
            pl.BlockSpec((None, 1, tn), lambda l, j: (l, 0, j)),
        ],
        out_specs=pl.BlockSpec((None, rows, tn), lambda l, j: (l, 0, j)),
        out_shape=jax.ShapeDtypeStruct((depth, rows, n6), F32),
        compiler_params=_cparams(("arbitrary", "arbitrary")),
        name="ada_mod",
    )(cvecs, w_mod, b_mod.reshape(depth, 1, n6))


def _mod_row_map(n_prompt_tiles, tiles_per_latent_seq):
    def row(i):
        return jnp.where(i < n_prompt_tiles, 0, 1 + (i - n_prompt_tiles) // tiles_per_latent_seq)
    return row


def _norm_mod(x, g, mod_ref, shift_row, scale_row):
    ms = jnp.mean(x * x, axis=-1, keepdims=True)
    y = x * lax.rsqrt(ms + EPS) * g
    return y * (1.0 + mod_ref[scale_row:scale_row + 1, :]) + mod_ref[shift_row:shift_row + 1, :]


def _norm1_kernel(x_ref, g_ref, mod_ref, o_ref):
    o_ref[...] = _norm_mod(x_ref[...], g_ref[...], mod_ref, 0, 1).astype(o_ref.dtype)


def _norm1(x, g_all, layer, mod_l, row_map):
    n, d = x.shape
    tt = TOKEN_TILE
    return pl.pallas_call(
        _norm1_kernel,
        grid=(n // tt,),
        in_specs=[
            pl.BlockSpec((tt, d), lambda i: (i, 0)),
            pl.BlockSpec((None, 1, d), lambda i: (layer, 0, 0)),
            pl.BlockSpec((None, 6, d), lambda i: (row_map(i), 0, 0)),
        ],
        out_specs=pl.BlockSpec((tt, d), lambda i: (i, 0)),
        out_shape=jax.ShapeDtypeStruct((n, d), BF16),
        compiler_params=_cparams(("arbitrary",)),
        name="norm1_mod",
    )(x, g_all, mod_l)


def _norm2_router_kernel(x_ref, g_ref, mod_ref, wh_ref, wl_ref, br_ref, hn_ref, eid_ref, wts_ref,
                         cnt_ref, *, n_groups, per_group):
    y = _norm_mod(x_ref[...], g_ref[...], mod_ref, 3, 4)
    hn_ref[...] = _pack_bf16_pairs(y)
    y_hi = y.astype(BF16)
    y_lo = (y - y_hi.astype(F32)).astype(BF16)
    w_hi = wh_ref[...]
    logits = (jnp.dot(y_hi, w_hi, preferred_element_type=F32)
              + (jnp.dot(y_lo, w_hi, preferred_element_type=F32)
                 + jnp.dot(y_hi, wl_ref[...], preferred_element_type=F32))) + br_ref[...]
    lane = lax.broadcasted_iota(jnp.int32, logits.shape, 1).astype(F32)
    big = float(LANES)
    gmask = lane < n_groups
    glog = jnp.where(gmask, logits, NEG_INF)
    gmax = jnp.max(glog, axis=-1, keepdims=True)
    gsum = jnp.sum(jnp.where(gmask, jnp.exp(glog - gmax), 0.0), axis=-1, keepdims=True)
    g_p = 1.0 / gsum
    g_i = jnp.min(jnp.where(gmask & (glog == gmax), lane, big), axis=-1, keepdims=True)
    lo = n_groups + g_i * per_group
    rmask = (lane >= lo) & (lane < lo + per_group)
    rlog = jnp.where(rmask, logits, NEG_INF)
    m1 = jnp.max(rlog, axis=-1, keepdims=True)
    i1 = jnp.min(jnp.where(rmask & (rlog == m1), lane, big), axis=-1, keepdims=True)
    rmask2 = rmask & (lane != i1)
    rlog2 = jnp.where(rmask2, logits, NEG_INF)
    m2 = jnp.max(rlog2, axis=-1, keepdims=True)
    i2 = jnp.min(jnp.where(rmask2 & (rlog2 == m2), lane, big), axis=-1, keepdims=True)
    e = jnp.exp(m2 - m1)
    w1 = g_p / (1.0 + e)
    w2 = w1 * e
    wts_ref[...] = jnp.where(lane == 0, w1, jnp.where(lane == 1, w2, 0.0))
    tt = logits.shape[0]

    @pl.when(pl.program_id(0) == 0)
    def _():
        cnt_ref[...] = jnp.zeros(cnt_ref.shape, F32)

    onehot = jnp.where((lane == i1) | (lane == i2), 1.0, 0.0)
    earlier = (lax.broadcasted_iota(jnp.int32, (tt, tt), 1)
               < lax.broadcasted_iota(jnp.int32, (tt, tt), 0))
    prefix = jnp.dot(jnp.where(earlier, 1.0, 0.0).astype(BF16), onehot.astype(BF16),
                     preferred_element_type=F32)
    before = cnt_ref[...] + prefix
    rank1 = jnp.sum(jnp.where(lane == i1, before, 0.0), axis=-1, keepdims=True)
    rank2 = jnp.sum(jnp.where(lane == i2, before, 0.0), axis=-1, keepdims=True)
    cnt_ref[...] += jnp.sum(onehot, axis=0, keepdims=True)
    eid_ref[...] = jnp.where(lane == 0, i1 - n_groups,
                             jnp.where(lane == 1, i2 - n_groups,
                                       jnp.where(lane == 2, rank1,
                                                 jnp.where(lane == 3, rank2, 0.0)))).astype(jnp.int32)


def _norm2_router(x, g_all, layer, mod_l, row_map, w_rt_hi, w_rt_lo, b_rt, n_groups, per_group):
    n, d = x.shape
    tt = TOKEN_TILE
    kern = functools.partial(_norm2_router_kernel, n_groups=n_groups, per_group=per_group)
    return pl.pallas_call(
        kern,
        grid=(n // tt,),
        in_specs=[
            pl.BlockSpec((tt, d), lambda i: (i, 0)),
            pl.BlockSpec((None, 1, d), lambda i: (layer, 0, 0)),
            pl.BlockSpec((None, 6, d), lambda i: (row_map(i), 0, 0)),
            pl.BlockSpec((d, LANES), lambda i: (0, 0)),
            pl.BlockSpec((d, LANES), lambda i: (0, 0)),
            pl.BlockSpec((1, LANES), lambda i: (0, 0)),
        ],
        out_specs=[
            pl.BlockSpec((tt, d // 2), lambda i: (i, 0)),
            pl.BlockSpec((tt, LANES), lambda i: (i, 0)),
            pl.BlockSpec((tt, LANES), lambda i: (i, 0)),
            pl.BlockSpec((1, LANES), lambda i: (0, 0)),
        ],
        out_shape=[
            jax.ShapeDtypeStruct((n, d // 2), jnp.uint32),
            jax.ShapeDtypeStruct((n, LANES), jnp.int32),
            jax.ShapeDtypeStruct((n, LANES), F32),
            jax.ShapeDtypeStruct((1, LANES), F32),
        ],
        compiler_params=_cparams(("arbitrary",)),
        name="norm2_router",
    )(x, g_all, mod_l, w_rt_hi, w_rt_lo, b_rt)


def _final_norm_kernel(x_ref, g_ref, o_ref):
    x = x_ref[...]
    ms = jnp.mean(x * x, axis=-1, keepdims=True)
    o_ref[...] = x * lax.rsqrt(ms + EPS) * g_ref[...]


def _final_norm(x, g):
    n, d = x.shape
    tt = TOKEN_TILE
    return pl.pallas_call(
        _final_norm_kernel,
        grid=(n // tt,),
        in_specs=[pl.BlockSpec((tt, d), lambda i: (i, 0)),
                  pl.BlockSpec((1, d), lambda i: (0, 0))],
        out_specs=pl.BlockSpec((tt, d), lambda i: (i, 0)),
        out_shape=jax.ShapeDtypeStruct((n, d), F32),
        compiler_params=_cparams(("arbitrary",)),
        name="final_norm",
    )(x, g.reshape(1, d))


def _mm_kernel(x_ref, w_ref, o_ref, wbf_ref):
    @pl.when(pl.program_id(1) == 0)
    def _():
        _cast_rows(w_ref, wbf_ref, 512)

    o_ref[...] = jnp.dot(x_ref[...], wbf_ref[...], preferred_element_type=F32).astype(o_ref.dtype)


def _in_proj(hn, w_in_all, layer):
    m, k = hn.shape
    n = w_in_all.shape[2]
    tn = _largest_tile(n, (1024, 512, 256))
    tm = _largest_tile(m, (1024, 512, 256))
    return pl.pallas_call(
        _mm_kernel,
        grid=(n // tn, m // tm),
        in_specs=[
            pl.BlockSpec((tm, k), lambda j, i: (i, 0)),
            pl.BlockSpec((None, k, tn), lambda j, i: (layer, 0, j), pipeline_mode=pl.Buffered(1)),
        ],
        out_specs=pl.BlockSpec((tm, tn), lambda j, i: (i, j)),
        out_shape=jax.ShapeDtypeStruct((m, n), F32),
        scratch_shapes=[pltpu.VMEM((k, tn), BF16)],
        compiler_params=_cparams(("arbitrary", "arbitrary")),
        name="in_proj",
    )(hn, w_in_all)


MERGE_W_CHUNK = 256
MERGE_K_SPLIT = 2


def _merge_kernel(oap_ref, obp_ref, ocp_ref, oas_ref, obs_ref, ocs_ref, ga_ref, gb_ref, gc_ref, w_hbm,
                  o_ref, wbf, stage, sem, acc, *, layer, n_ctx_tiles):
    dk = oap_ref.shape[1]
    d = wbf.shape[0] // 3
    tn = o_ref.shape[1]
    ck = stage.shape[1]
    n_ck = wbf.shape[0] // ck
    kk = pl.program_id(2)

    @pl.when((pl.program_id(1) == 0) & (kk == 0))
    def _():
        col0 = pl.multiple_of(pl.program_id(0) * tn, tn)

        def copy(c, slot):
            r0 = pl.multiple_of(c * ck, ck)
            return pltpu.make_async_copy(w_hbm.at[layer, pl.ds(r0, ck), pl.ds(col0, tn)],
                                         stage.at[slot], sem.at[slot])

        copy(0, 0).start()

        def body(c, carry):
            slot = c & 1

            @pl.when(c + 1 < n_ck)
            def _():
                copy(c + 1, 1 - slot).start()

            copy(c, slot).wait()
            r0 = pl.multiple_of(c * ck, ck)
            wbf[pl.ds(r0, ck), :] = stage[slot].astype(BF16)
            return carry

        lax.fori_loop(0, n_ck, body, 0)

    def part(o_ref_, r):
        k0 = pl.multiple_of(r * d + kk * dk, dk)
        return jnp.dot(o_ref_[...], wbf[pl.ds(k0, dk), :], preferred_element_type=F32)

    def gate(g_ref_):
        return 0.5 * jnp.tanh(0.5 * g_ref_[...]) + 0.5

    def merge(oa_ref, ob_ref, oc_ref):
        branches = ((oa_ref, ga_ref), (ob_ref, gb_ref), (oc_ref, gc_ref))

        @pl.when(kk == 0)
        def _():
            for r, (x_ref, _) in enumerate(branches):
                acc[r] = part(x_ref, r)

        @pl.when((kk > 0) & (kk < MERGE_K_SPLIT - 1))
        def _():
            for r, (x_ref, _) in enumerate(branches):
                acc[r] += part(x_ref, r)

        @pl.when(kk == MERGE_K_SPLIT - 1)
        def _():
            out = None
            for r, (x_ref, g_ref_) in enumerate(branches):
                term = gate(g_ref_) * (acc[r] + part(x_ref, r))
                out = term if out is None else out + term
            o_ref[...] = out.astype(o_ref.dtype)

    @pl.when(pl.program_id(1) < n_ctx_tiles)
    def _():
        merge(oap_ref, obp_ref, ocp_ref)

    @pl.when(pl.program_id(1) >= n_ctx_tiles)
    def _():
        merge(oas_ref, obs_ref, ocs_ref)


def _merge(mix_ctx, mix_lat, proj, gate_off, w_branch_all, layer):
    n_p, d = mix_ctx[0].shape
    m = n_p + mix_lat[0].shape[0]
    n = w_branch_all.shape[2]
    tn = _largest_tile(math.gcd(n, gate_off), (1024, 512, 256))
    tm = TOKEN_TILE
    ck = MERGE_W_CHUNK
    nk = MERGE_K_SPLIT
    dk = d // nk
    assert nk >= 2 and d % nk == 0 and dk % ck == 0 and n_p % tm == 0
    gb = gate_off // tn
    nb = n // tn
    npt = n_p // tm
    cspec = pl.BlockSpec((tm, dk), lambda j, i, k: (jnp.minimum(i, npt - 1),
                                                    jnp.where(i < npt, k, nk - 1)))
    lspec = pl.BlockSpec((tm, dk), lambda j, i, k: (jnp.maximum(i - npt, 0),
                                                    jnp.where(i >= npt, k, 0)))
    gspec = lambda r: pl.BlockSpec((tm, tn), lambda j, i, k: (i, gb + r * nb + j))
    return pl.pallas_call(
        functools.partial(_merge_kernel, layer=layer, n_ctx_tiles=npt),
        grid=(nb, m // tm, nk),
        in_specs=[cspec, cspec, cspec, lspec, lspec, lspec, gspec(0), gspec(1), gspec(2),
                  pl.BlockSpec(memory_space=pl.ANY)],
        out_specs=pl.BlockSpec((tm, tn), lambda j, i, k: (i, j)),
        out_shape=jax.ShapeDtypeStruct((m, n), BF16),
        scratch_shapes=[pltpu.VMEM((3 * d, tn), BF16), pltpu.VMEM((2, ck, tn), F32),
                        pltpu.SemaphoreType.DMA((2,)), pltpu.VMEM((3, tm, tn), F32)],
        compiler_params=_cparams(("arbitrary", "arbitrary", "arbitrary")),
        name="branch_merge",
    )(*mix_ctx, *mix_lat, proj, proj, proj, w_branch_all)


def _out_proj_kernel(h_ref, w_ref, x_ref, mod_ref, o_ref, wbf_ref):
    @pl.when(pl.program_id(1) == 0)
    def _():
        _cast_rows(w_ref, wbf_ref, 512)

    mix = jnp.dot(h_ref[...], wbf_ref[...], preferred_element_type=F32)
    o_ref[...] = x_ref[...] + mod_ref[2:3, :] * mix


def _out_proj(merged, w_out_all, layer, x, mod_l, row_map):
    m, k = merged.shape
    n = w_out_all.shape[2]
    tn = _largest_tile(n, (1024, 512, 256))
    tm = TOKEN_TILE
    return pl.pallas_call(
        _out_proj_kernel,
        grid=(n // tn, m // tm),
        in_specs=[
            pl.BlockSpec((tm, k), lambda j, i: (i, 0)),
            pl.BlockSpec((None, k, tn), lambda j, i: (layer, 0, j), pipeline_mode=pl.Buffered(1)),
            pl.BlockSpec((tm, tn), lambda j, i: (i, j)),
            pl.BlockSpec((None, 6, tn), lambda j, i: (row_map(i), 0, j)),
        ],
        out_specs=pl.BlockSpec((tm, tn), lambda j, i: (i, j)),
        out_shape=jax.ShapeDtypeStruct((m, n), F32),
        scratch_shapes=[pltpu.VMEM((k, tn), BF16)],
        compiler_params=_cparams(("arbitrary", "arbitrary")),
        name="out_proj",
    )(merged, w_out_all, x, mod_l)


RG_CHUNK = 128


def _rglru_kernel(rx_ref, rg_ref, cw_ref, cb_ref, waf_ref, wxf_ref, wab_ref, wxb_ref,
                  ba_ref, bx_ref, lam_ref, h0_ref, o_ref, st_ref,
                  af, uf, ab, ub, wbf, *, seq):
    tc = RG_CHUNK
    n_chunks = seq // tc
    c = rx_ref.shape[1]
    wbf[0] = waf_ref[...].astype(BF16)
    wbf[1] = wxf_ref[...].astype(BF16)
    wbf[2] = wab_ref[...].astype(BF16)
    wbf[3] = wxb_ref[...].astype(BF16)
    cw = cw_ref[...]
    cb = cb_ref[...]

    def softplus(z):
        return jnp.maximum(z, 0.0) + jnp.log1p(jnp.exp(-jnp.abs(z)))

    sp_f = softplus(-lam_ref[0:1, :])
    sp_b = softplus(-lam_ref[1:2, :])
    row8 = lax.broadcasted_iota(jnp.int32, (tc, c), 0) & (SUBLANES - 1)

    def prep(k, carry):
        r0 = pl.multiple_of(k * tc, tc)
        cur = rx_ref[pl.ds(r0, tc), :]
        p0 = pl.multiple_of(jnp.maximum(r0 - SUBLANES, 0), SUBLANES)
        n0 = pl.multiple_of(jnp.minimum(r0 + tc, seq - SUBLANES), SUBLANES)
        prev8 = jnp.where(k > 0, rx_ref[pl.ds(p0, SUBLANES), :], 0.0)
        next8 = jnp.where(k < n_chunks - 1, rx_ref[pl.ds(n0, SUBLANES), :], 0.0)
        ext = jnp.concatenate([prev8, cur, next8], axis=0)
        ne = tc + 2 * SUBLANES
        xm2 = pltpu.roll(ext, 2, 0)[SUBLANES:SUBLANES + tc]
        xm1 = pltpu.roll(ext, 1, 0)[SUBLANES:SUBLANES + tc]
        xp1 = pltpu.roll(ext, ne - 1, 0)[SUBLANES:SUBLANES + tc]
        xc = cb + xm2 * cw[0:1] + xm1 * cw[1:2] + cur * cw[2:3] + xp1 * cw[3:4]
        xcb = xc.astype(BF16)

        def gates(wi, d, sp):
            r = _sigmoid(jnp.dot(xcb, wbf[wi], preferred_element_type=F32) + ba_ref[d:d + 1, :])
            i = _sigmoid(jnp.dot(xcb, wbf[wi + 1], preferred_element_type=F32) + bx_ref[d:d + 1, :])
            log_a = (-RG_C) * r * sp
            a = jnp.exp(log_a)
            u = jnp.sqrt(1.0 - a * a) * i * xc
            return a, u

        a, u = gates(0, 0, sp_f)
        for dd in (1, 2, 4):
            msk = row8 >= dd
            a_s = jnp.where(msk, _roll8(a, dd), 1.0)
            u_s = jnp.where(msk, _roll8(u, dd), 0.0)
            u = a * u_s + u
            a = a * a_s
        af[pl.ds(r0, tc), :] = a
        uf[pl.ds(r0, tc), :] = u

        a, u = gates(2, 1, sp_b)
        for dd in (1, 2, 4):
            msk = row8 < SUBLANES - dd
            a_s = jnp.where(msk, _roll8(a, SUBLANES - dd), 1.0)
            u_s = jnp.where(msk, _roll8(u, SUBLANES - dd), 0.0)
            u = a * u_s + u
            a = a * a_s
        ab[pl.ds(r0, tc), :] = a
        ub[pl.ds(r0, tc), :] = u
        return carry

    lax.fori_loop(0, n_chunks, prep, 0)

    n8 = seq // SUBLANES

    def carry_step(k, carry):
        hf, hb = carry
        i0 = pl.multiple_of(k * SUBLANES, SUBLANES)
        hf_t = af[pl.ds(i0, SUBLANES), :] * hf + uf[pl.ds(i0, SUBLANES), :]
        uf[pl.ds(i0, SUBLANES), :] = hf_t
        j0 = pl.multiple_of((n8 - 1 - k) * SUBLANES, SUBLANES)
        hb_t = ab[pl.ds(j0, SUBLANES), :] * hb + ub[pl.ds(j0, SUBLANES), :]
        ub[pl.ds(j0, SUBLANES), :] = hb_t
        return hf_t[SUBLANES - 1:SUBLANES, :], hb_t[0:1, :]

    hf, hb = lax.fori_loop(0, n8, carry_step, (h0_ref[0:1, :], h0_ref[1:2, :]))
    st_ref[0:1, :] = hf
    st_ref[1:2, :] = hb

    def finish(k, carry):
        r0 = pl.multiple_of(k * tc, tc)
        hsum = uf[pl.ds(r0, tc), :] + ub[pl.ds(r0, tc), :]
        o_ref[pl.ds(r0, tc), :] = (hsum * _gelu_tanh(rg_ref[pl.ds(r0, tc), :])).astype(o_ref.dtype)
        return carry

    lax.fori_loop(0, n_chunks, finish, 0)


def _rglru(proj, row_blk0, n_seq, seq, d, conv_w, conv_b, rg_wa, rg_wx, rg_ba, rg_bx, rg_lambda,
           h0, layer):
    nb, c = rg_wa.shape[2], rg_wa.shape[3]
    kern = functools.partial(_rglru_kernel, seq=seq)
    wspec = lambda direction: pl.BlockSpec((None, None, None, c, c),
                                           lambda b, n: (layer, direction, n, 0, 0))
    vec2 = pl.BlockSpec((None, 2, c), lambda b, n: (layer, 0, n))
    return pl.pallas_call(
        kern,
        grid=(n_seq, nb),
        in_specs=[
            pl.BlockSpec((seq, c), lambda b, n: (row_blk0 + b, n)),
            pl.BlockSpec((seq, c), lambda b, n: (row_blk0 + b, nb + n)),
            pl.BlockSpec((None, conv_w.shape[1], c), lambda b, n: (layer, 0, n)),
            pl.BlockSpec((None, 1, c), lambda b, n: (layer, 0, n)),
            wspec(0), wspec(0), wspec(1), wspec(1),
            vec2, vec2, vec2,
            pl.BlockSpec((None, 2, c), lambda b, n: (b, 0, n)),
        ],
        out_specs=[
            pl.BlockSpec((seq, c), lambda b, n: (b, n)),
            pl.BlockSpec((None, 2, c), lambda b, n: (b, 0, n)),
        ],
        out_shape=[
            jax.ShapeDtypeStruct((n_seq * seq, d), BF16),
            jax.ShapeDtypeStruct((n_seq, 2, d), F32),
        ],
        scratch_shapes=[pltpu.VMEM((seq, c), F32)] * 4 + [pltpu.VMEM((4, c, c), BF16)],
        compiler_params=_cparams(("arbitrary", "arbitrary")),
        name="rglru",
    )(proj, proj, conv_w, conv_b, rg_wa, rg_wx, rg_wa, rg_wx, rg_ba, rg_bx, rg_lambda, h0)


def _rope(x, cos_f, sin_f):
    return x * cos_f + pltpu.roll(x, HEAD_DIM // 2, 1) * sin_f


def _sink_softmax_pv(s, sink_col, v):
    m = jnp.maximum(jnp.max(s, axis=-1, keepdims=True), sink_col)
    p = jnp.exp(s - m)
    denom = jnp.sum(p, axis=-1, keepdims=True) + jnp.exp(sink_col - m)
    o = jnp.dot(p.astype(BF16), v, preferred_element_type=F32)
    return o / denom


def _sink_col(sink_ref, layer, hkv, group, rows_per_head):
    rows = group * rows_per_head
    assert rows_per_head & (rows_per_head - 1) == 0
    head = lax.broadcasted_iota(jnp.int32, (rows, 1), 0) >> (rows_per_head.bit_length() - 1)
    col = jnp.zeros((rows, 1), F32)
    for g in range(group):
        col = jnp.where(head == g, sink_ref[layer, hkv * group + g], col)
    return col


WIN_KV_PER_STEP = 2


def _kv_per_step(n_kv, *col_offsets, unit):
    kvs = WIN_KV_PER_STEP if n_kv % WIN_KV_PER_STEP == 0 else 1
    if any(off % (kvs * unit) for off in col_offsets):
        kvs = 1
    return kvs


def _win_ctx_kernel(sink_ref, q_ref, k_ref, v_ref, o_ref, *, layer, group, kvs):
    tq = q_ref.shape[0]
    hd = HEAD_DIM
    gw = group * hd
    for hh in range(kvs):
        hkv = pl.program_id(1) * kvs + hh
        q = jnp.concatenate([q_ref[:, hh * gw + g * hd:hh * gw + (g + 1) * hd] * ATTN_SCALE
                             for g in range(group)], axis=0)
        s = lax.dot_general(q.astype(BF16), k_ref[:, hh * hd:(hh + 1) * hd].astype(BF16), _DN_QKT,
                            preferred_element_type=F32)
        o = _sink_softmax_pv(s, _sink_col(sink_ref, layer, hkv, group, tq),
                             v_ref[:, hh * hd:(hh + 1) * hd].astype(BF16))
        for g in range(group):
            o_ref[:, hh * gw + g * hd:hh * gw + (g + 1) * hd] = o[g * tq:(g + 1) * tq].astype(o_ref.dtype)


def _win_ctx(proj, n_seq, seq, q_off, k_off, v_off, n_kv, group, win_sink, layer):
    gw = group * HEAD_DIM
    kvs = _kv_per_step(n_kv, k_off, v_off, unit=HEAD_DIM)
    kw = kvs * HEAD_DIM
    assert q_off % (kvs * gw) == 0
    kern = functools.partial(_win_ctx_kernel, layer=layer, group=group, kvs=kvs)
    return pl.pallas_call(
        kern,
        grid_spec=pltpu.PrefetchScalarGridSpec(
            num_scalar_prefetch=1,
            grid=(n_seq, n_kv // kvs),
            in_specs=[
                pl.BlockSpec((seq, kvs * gw), lambda b, h, s: (b, q_off // (kvs * gw) + h)),
                pl.BlockSpec((seq, kw), lambda b, h, s: (b, k_off // kw + h)),
                pl.BlockSpec((seq, kw), lambda b, h, s: (b, v_off // kw + h)),
            ],
            out_specs=pl.BlockSpec((seq, kvs * gw), lambda b, h, s: (b, h)),
        ),
        out_shape=jax.ShapeDtypeStruct((n_seq * seq, n_kv * gw), BF16),
        compiler_params=_cparams(("arbitrary", "arbitrary")),
        name="win_attn_ctx",
    )(win_sink, proj, proj, proj)


def _win_lat_kernel(sink_ref, q_ref, kp_ref, kc_ref, kn_ref, vp_ref, vc_ref, vn_ref, ck_ref, cv_ref,
                    cq_ref, sq_ref, cp_ref, sp_ref, cc_ref, sc_ref, cn_ref, sn_ref, o_ref, *,
                    layer, group, n_qblk, kvs):
    qi = pl.program_id(2)
    tq = q_ref.shape[0]
    past = ck_ref.shape[0]
    hd = HEAD_DIM
    gw = group * hd
    cq, sq = cq_ref[...], sq_ref[...]
    cp, sp, cc, sc, cn, sn = (cp_ref[...], sp_ref[...], cc_ref[...], sc_ref[...], cn_ref[...],
                              sn_ref[...])
    shape = (group * tq, past + 3 * tq)
    r = lax.broadcasted_iota(jnp.int32, shape, 0) & (tq - 1)
    c = lax.broadcasted_iota(jnp.int32, shape, 1) - past
    in_prev = (c >= 0) & (c < tq)
    in_next = c >= 2 * tq
    bad = (in_prev & ((c < r + (tq - WINDOW)) | (qi == 0))) | \
          (in_next & ((c - 2 * tq > r - (tq - WINDOW)) | (qi == n_qblk - 1)))
    for hh in range(kvs):
        hkv = pl.program_id(1) * kvs + hh
        kcol = slice(hh * hd, (hh + 1) * hd)
        q = jnp.concatenate(
            [_rope(q_ref[:, hh * gw + g * hd:hh * gw + (g + 1) * hd], cq, sq) * ATTN_SCALE
             for g in range(group)], axis=0).astype(BF16)
        k = jnp.concatenate([
            ck_ref[:, kcol].astype(BF16),
            _rope(kp_ref[:, kcol], cp, sp).astype(BF16),
            _rope(kc_ref[:, kcol], cc, sc).astype(BF16),
            _rope(kn_ref[:, kcol], cn, sn).astype(BF16)], axis=0)
        v = jnp.concatenate([cv_ref[:, kcol].astype(BF16), vp_ref[:, kcol].astype(BF16),
                             vc_ref[:, kcol].astype(BF16), vn_ref[:, kcol].astype(BF16)], axis=0)
        s = lax.dot_general(q, k, _DN_QKT, preferred_element_type=F32)
        s = jnp.where(bad, NEG_INF, s)
        o = _sink_softmax_pv(s, _sink_col(sink_ref, layer, hkv, group, tq), v)
        for g in range(group):
            o_ref[:, hh * gw + g * hd:hh * gw + (g + 1) * hd] = o[g * tq:(g + 1) * tq].astype(o_ref.dtype)


def _win_lat(proj, row0, n_seq, seq, q_off, k_off, v_off, n_kv, group, win_sink, layer,
             cache_k, cache_v, cos_f, sin_f):
    tq = WINDOW
    assert seq % tq == 0 and row0 % tq == 0
    nq = seq // tq
    gw = group * HEAD_DIM
    kvs = _kv_per_step(n_kv, k_off, v_off, unit=HEAD_DIM)
    kw = kvs * HEAD_DIM
    assert q_off % (kvs * gw) == 0
    past = cache_k.shape[2]
    rb0 = row0 // tq
    kb, vb = k_off // kw, v_off // kw
    kern = functools.partial(_win_lat_kernel, layer=layer, group=group, n_qblk=nq, kvs=kvs)
    prv = lambda i: jnp.maximum(i - 1, 0)
    nxt = lambda i: jnp.minimum(i + 1, nq - 1)
    kv = lambda colb, f: pl.BlockSpec((tq, kw), lambda b, h, i, s: (rb0 + b * nq + f(i), colb + h))
    cache = pl.BlockSpec((None, None, past, kw), lambda b, h, i, s: (b, layer, 0, h))
    tab = lambda f: pl.BlockSpec((tq, HEAD_DIM), lambda b, h, i, s: (f(i), 0))
    same = lambda i: i
    return pl.pallas_call(
        kern,
        grid_spec=pltpu.PrefetchScalarGridSpec(
            num_scalar_prefetch=1,
            grid=(n_seq, n_kv // kvs, nq),
            in_specs=[
                pl.BlockSpec((tq, kvs * gw),
                             lambda b, h, i, s: (rb0 + b * nq + i, q_off // (kvs * gw) + h)),
                kv(kb, prv), kv(kb, same), kv(kb, nxt),
                kv(vb, prv), kv(vb, same), kv(vb, nxt),
                cache, cache,
                tab(same), tab(same), tab(prv), tab(prv), tab(same), tab(same), tab(nxt), tab(nxt),
            ],
            out_specs=pl.BlockSpec((tq, kvs * gw), lambda b, h, i, s: (b * nq + i, h)),
        ),
        out_shape=jax.ShapeDtypeStruct((n_seq * seq, n_kv * gw), BF16),
        compiler_params=_cparams(("arbitrary", "arbitrary", "arbitrary")),
        name="win_attn_lat",
    )(win_sink, proj, proj, proj, proj, proj, proj, proj, cache_k, cache_v,
      cos_f, sin_f, cos_f, sin_f, cos_f, sin_f, cos_f, sin_f)


def _diff_lambda(l4, lam_init):
    t1 = jnp.sum(l4[0:1, :] * l4[1:2, :], axis=-1, keepdims=True)
    t2 = jnp.sum(l4[2:3, :] * l4[3:4, :], axis=-1, keepdims=True)
    return jnp.exp(t1) - jnp.exp(t2) + lam_init


ATTN_SCALE = HEAD_DIM ** -0.5
_DN_QKT = (((1,), (1,)), ((), ()))


def _subln(o, g, lam_init):
    ms = jnp.mean(o * o, axis=-1, keepdims=True)
    return o * lax.rsqrt(ms + EPS) * g * (1.0 - lam_init)


def _diff_core(q0, q1, k0, k1, v, lam, g, lam_init):
    s0 = lax.dot_general(q0, k0, _DN_QKT, preferred_element_type=F32)
    s1 = lax.dot_general(q1, k1, _DN_QKT, preferred_element_type=F32)
    e0 = jnp.exp(s0 - jnp.max(s0, axis=-1, keepdims=True))
    e1 = jnp.exp(s1 - jnp.max(s1, axis=-1, keepdims=True))
    c0 = 1.0 / jnp.sum(e0, axis=-1, keepdims=True)
    c1 = lam / jnp.sum(e1, axis=-1, keepdims=True)
    a = e0 * c0 - e1 * c1
    o = jnp.dot(a.astype(BF16), v, preferred_element_type=F32)
    return _subln(o, g, lam_init)


def _diff_ctx_kernel(q_ref, k_ref, v_ref, l4_ref, g_ref, o_ref, *, lam_init):
    lam = _diff_lambda(l4_ref[...], lam_init)
    hd = HEAD_DIM
    for h in range(q_ref.shape[1] // (2 * hd)):
        c0, c1, c2 = 2 * h * hd, (2 * h + 1) * hd, (2 * h + 2) * hd
        o = _diff_core((q_ref[:, c0:c1] * ATTN_SCALE).astype(BF16),
                       (q_ref[:, c1:c2] * ATTN_SCALE).astype(BF16),
                       k_ref[:, c0:c1].astype(BF16), k_ref[:, c1:c2].astype(BF16),
                       v_ref[:, c0:c2].astype(BF16), lam, g_ref[...], lam_init)
        o_ref[:, c0:c2] = o.astype(o_ref.dtype)


DIFF_CTX_HEADS_PER_STEP = 4


def _diff_ctx(proj, n_seq, seq, q_off, k_off, v_off, n_heads, diff_lambda, subln_g, layer, lam_init):
    hps = DIFF_CTX_HEADS_PER_STEP if n_heads % DIFF_CTX_HEADS_PER_STEP == 0 else 1
    hw = 2 * HEAD_DIM
    bw = hps * hw
    kern = functools.partial(_diff_ctx_kernel, lam_init=lam_init)
    blk = lambda off: pl.BlockSpec((seq, bw), lambda b, h: (b, off // bw + h))
    assert q_off % bw == 0 and k_off % bw == 0 and v_off % bw == 0
    return pl.pallas_call(
        kern,
        grid=(n_seq, n_heads // hps),
        in_specs=[blk(q_off), blk(k_off), blk(v_off),
                  pl.BlockSpec((None, 4, HEAD_DIM), lambda b, h: (layer, 0, 0)),
                  pl.BlockSpec((None, 1, hw), lambda b, h: (layer, 0, 0))],
        out_specs=pl.BlockSpec((seq, bw), lambda b, h: (b, h)),
        out_shape=jax.ShapeDtypeStruct((n_seq * seq, n_heads * hw), BF16),
        compiler_params=_cparams(("arbitrary", "arbitrary")),
        name="diff_attn_ctx",
    )(proj, proj, proj, diff_lambda, subln_g)


DIFF_KV_CHUNK = 256


def _diff_lat_kernel(q_ref, k_ref, v_ref, ck_ref, cv_ref, cq_ref, sq_ref, call_ref, sall_ref,
                     l4_ref, g_ref, o_ref, kall, vall, *, lam_init):
    hd = HEAD_DIM
    past = ck_ref.shape[0]
    seq = k_ref.shape[0]

    @pl.when(pl.program_id(2) == 0)
    def _():
        kall[0:past, :] = ck_ref[...].astype(BF16)
        vall[0:past, :] = cv_ref[...].astype(BF16)

        def fill(i, carry):
            r0 = pl.multiple_of(i * DIFF_KV_CHUNK, DIFF_KV_CHUNK)
            rows = pl.ds(r0, DIFF_KV_CHUNK)
            dst = pl.ds(past + r0, DIFF_KV_CHUNK)
            cs, sn = call_ref[rows, :], sall_ref[rows, :]
            kall[dst, 0:hd] = _rope(k_ref[rows, 0:hd], cs, sn).astype(BF16)
            kall[dst, hd:2 * hd] = _rope(k_ref[rows, hd:2 * hd], cs, sn).astype(BF16)
            vall[dst, :] = v_ref[rows, :].astype(BF16)
            return carry

        lax.fori_loop(0, seq // DIFF_KV_CHUNK, fill, 0)

    lam = _diff_lambda(l4_ref[...], lam_init)
    cq, sq = cq_ref[...], sq_ref[...]
    q0 = (_rope(q_ref[:, 0:hd], cq, sq) * ATTN_SCALE).astype(BF16)
    q1 = (_rope(q_ref[:, hd:2 * hd], cq, sq) * ATTN_SCALE).astype(BF16)
    o = _diff_core(q0, q1, kall[:, 0:hd], kall[:, hd:2 * hd], vall[...], lam, g_ref[...], lam_init)
    o_ref[...] = o.astype(o_ref.dtype)


def _diff_lat(proj, row0, n_seq, seq, q_off, k_off, v_off, n_heads, diff_lambda, subln_g, layer,
              lam_init, cache_k, cache_v, cos_f, sin_f):
    hw = 2 * HEAD_DIM
    tq = 256
    kc = DIFF_KV_CHUNK
    assert seq % kc == 0 and row0 % seq == 0 and seq % tq == 0
    nq = seq // tq
    past = cache_k.shape[2]
    sb0 = row0 // seq
    kern = functools.partial(_diff_lat_kernel, lam_init=lam_init)
    full = lambda off: pl.BlockSpec((seq, hw), lambda b, h, i: (sb0 + b, off // hw + h))
    cache = pl.BlockSpec((None, None, past, hw), lambda b, h, i: (b, layer, 0, h))
    return pl.pallas_call(
        kern,
        grid=(n_seq, n_heads, nq),
        in_specs=[
            pl.BlockSpec((tq, hw), lambda b, h, i: (sb0 * nq + b * nq + i, q_off // hw + h)),
            full(k_off), full(v_off), cache, cache,
            pl.BlockSpec((tq, HEAD_DIM), lambda b, h, i: (i, 0)),
            pl.BlockSpec((tq, HEAD_DIM), lambda b, h, i: (i, 0)),
            pl.BlockSpec((seq, HEAD_DIM), lambda b, h, i: (0, 0)),
            pl.BlockSpec((seq, HEAD_DIM), lambda b, h, i: (0, 0)),
            pl.BlockSpec((None, 4, HEAD_DIM), lambda b, h, i: (layer, 0, 0)),
            pl.BlockSpec((None, 1, hw), lambda b, h, i: (layer, 0, 0)),
        ],
        out_specs=pl.BlockSpec((tq, hw), lambda b, h, i: (b * nq + i, h)),
        out_shape=jax.ShapeDtypeStruct((n_seq * seq, n_heads * hw), BF16),
        scratch_shapes=[pltpu.VMEM((past + seq, hw), BF16), pltpu.VMEM((past + seq, hw), BF16)],
        compiler_params=_cparams(("arbitrary", "arbitrary", "arbitrary")),
        name="diff_attn_lat",
    )(proj, proj, proj, cache_k, cache_v, cos_f, sin_f, cos_f, sin_f, diff_lambda, subln_g)


def _moe_plan(eid, rank, counts, n_experts, rows_per_blk):
    n = eid.shape[0]
    a = n * TOP_K
    flat_e = eid.reshape(-1)
    nblk_e = (counts + rows_per_blk - 1) // rows_per_blk
    blk_end = jnp.cumsum(nblk_e)
    blk_start = blk_end - nblk_e
    is_e = flat_e[:, None] == jnp.arange(n_experts, dtype=jnp.int32)[None, :]
    start_of = jnp.sum(jnp.where(is_e, blk_start[None, :], 0), axis=1)
    dest = (start_of * rows_per_blk + rank.reshape(-1)).astype(jnp.int32)
    n_blk = a // rows_per_blk + n_experts
    n_used = blk_end[-1].astype(jnp.int32)
    slot_tok = jnp.zeros((n_blk * rows_per_blk,), jnp.int32).at[dest].set(
        jnp.arange(a, dtype=jnp.int32) // TOP_K)
    blk_ids = jnp.arange(n_blk, dtype=jnp.int32)
    blk_e = jnp.minimum(jnp.searchsorted(blk_end, blk_ids, side='right'), n_experts - 1)
    blk_e = jnp.where(blk_ids < n_used, blk_e, blk_e[jnp.maximum(n_used - 1, 0)]).astype(jnp.int32)
    first = jnp.concatenate([jnp.ones((1,), jnp.int32),
                             (blk_e[1:] != blk_e[:-1]).astype(jnp.int32)])
    return slot_tok, dest, blk_e, first, n_used.reshape(1)


def _gather_rows_kernel(tok_ref, x_hbm, o_ref, buf, sem, *, rows_per_step):
    i = pl.program_id(0)
    slot = i & 1

    def issue(step, dst_slot):
        base = step * rows_per_step

        def body(r, carry):
            tok = tok_ref[base + r]
            pltpu.make_async_copy(x_hbm.at[pl.ds(tok, 1)], buf.at[dst_slot, pl.ds(r, 1)],
                                  sem.at[dst_slot]).start()
            return carry

        lax.fori_loop(0, rows_per_step, body, 0, unroll=DMA_ISSUE_UNROLL)

    @pl.when(i == 0)
    def _():
        issue(0, 0)

    @pl.when(i + 1 < pl.num_programs(0))
    def _():
        issue(i + 1, 1 - slot)

    pltpu.make_async_copy(x_hbm.at[pl.ds(0, rows_per_step)], buf.at[slot], sem.at[slot]).wait()
    w = buf.shape[2]
    lo, hi = _unpack_bf16_pairs(buf[slot])
    o_ref[:, 0:w] = lo.astype(o_ref.dtype)
    o_ref[:, w:2 * w] = hi.astype(o_ref.dtype)


def _gather_rows(x, slot_tok):
    rows = slot_tok.shape[0]
    w = x.shape[1]
    d = 2 * w
    rps = MOE_ROWS
    kern = functools.partial(_gather_rows_kernel, rows_per_step=rps)
    return pl.pallas_call(
        kern,
        grid_spec=pltpu.PrefetchScalarGridSpec(
            num_scalar_prefetch=1,
            grid=(rows // rps,),
            in_specs=[pl.BlockSpec(memory_space=pl.ANY)],
            out_specs=pl.BlockSpec((rps, d), lambda i, tok: (i, 0)),
            scratch_shapes=[pltpu.VMEM((2, rps, w), x.dtype), pltpu.SemaphoreType.DMA((2,))],
        ),
        out_shape=jax.ShapeDtypeStruct((rows, d), BF16),
        compiler_params=_cparams(("arbitrary",)),
        name="moe_gather",
    )(slot_tok, x)


def _gmm1_kernel(be_ref, first_ref, nused_ref, x_ref, wg_ref, wu_ref, o_ref, wgb, wub):
    blk = pl.program_id(1)

    @pl.when(first_ref[blk] == 1)
    def _():
        _cast_rows(wg_ref, wgb, 512)
        _cast_rows(wu_ref, wub, 512)

    @pl.when(blk < nused_ref[0])
    def _():
        x = x_ref[...]
        g = jnp.dot(x, wgb[...], preferred_element_type=F32)
        u = jnp.dot(x, wub[...], preferred_element_type=F32)
        o_ref[...] = (g * _sigmoid(g) * u).astype(o_ref.dtype)

    @pl.when(blk >= nused_ref[0])
    def _():
        o_ref[...] = jnp.zeros(o_ref.shape, o_ref.dtype)


def _gmm1(xs, w_gate_all, w_up_all, layer, blk_e, first, n_used):
    rows, d = xs.shape
    de = w_gate_all.shape[3]
    tn = _largest_tile(de, (512, 256, 128))
    r = MOE_ROWS
    wspec = pl.BlockSpec((None, None, d, tn), lambda j, b, be, fi, nu: (layer, be[b], 0, j))
    return pl.pallas_call(
        _gmm1_kernel,
        grid_spec=pltpu.PrefetchScalarGridSpec(
            num_scalar_prefetch=3,
            grid=(de // tn, rows // r),
            in_specs=[pl.BlockSpec((r, d), lambda j, b, be, fi, nu: (b, 0)), wspec, wspec],
            out_specs=pl.BlockSpec((r, tn), lambda j, b, be, fi, nu: (b, j)),
            scratch_shapes=[pltpu.VMEM((d, tn), BF16), pltpu.VMEM((d, tn), BF16)],
        ),
        out_shape=jax.ShapeDtypeStruct((rows, de), BF16),
        compiler_params=_cparams(("arbitrary", "arbitrary")),
        name="moe_gate_up",
    )(blk_e, first, n_used, xs, w_gate_all, w_up_all)


def _gmm2_kernel(be_ref, first_ref, nused_ref, h_ref, wd_ref, o_ref, wdb):
    blk = pl.program_id(1)

    @pl.when(first_ref[blk] == 1)
    def _():
        _cast_rows(wd_ref, wdb, 512)

    @pl.when(blk < nused_ref[0])
    def _():
        o_ref[...] = _pack_bf16_pairs(jnp.dot(h_ref[...], wdb[...], preferred_element_type=F32))

    @pl.when(blk >= nused_ref[0])
    def _():
        o_ref[...] = jnp.zeros(o_ref.shape, o_ref.dtype)


def _gmm2(h, w_down_all, layer, blk_e, first, n_used):
    rows, de = h.shape
    d = w_down_all.shape[3]
    tn = _largest_tile(d, (2048, 1024, 512, 256))
    r = MOE_ROWS
    ys = pl.pallas_call(
        _gmm2_kernel,
        grid_spec=pltpu.PrefetchScalarGridSpec(
            num_scalar_prefetch=3,
            grid=(d // tn, rows // r),
            in_specs=[
                pl.BlockSpec((r, de), lambda j, b, be, fi, nu: (b, 0)),
                pl.BlockSpec((None, None, de, tn), lambda j, b, be, fi, nu: (layer, be[b], 0, j)),
            ],
            out_specs=pl.BlockSpec((r, tn // 2), lambda j, b, be, fi, nu: (b, j)),
            scratch_shapes=[pltpu.VMEM((de, tn), BF16)],
        ),
        out_shape=jax.ShapeDtypeStruct((rows, d // 2), jnp.uint32),
        compiler_params=_cparams(("arbitrary", "arbitrary")),
        name="moe_down",
    )(blk_e, first, n_used, h, w_down_all)
    return ys, tn


def _combine_kernel(slot_ref, x_ref, wts_ref, mod_ref, ys_hbm, o_ref, buf, sem, *, pack_cols):
    tt = x_ref.shape[0]
    base = pl.program_id(0) * tt

    def issue(r, carry):
        for k in range(TOP_K):
            s = slot_ref[TOP_K * (base + r) + k]
            pltpu.make_async_copy(ys_hbm.at[pl.ds(s, 1)], buf.at[k, pl.ds(r, 1)], sem).start()
        return carry

    lax.fori_loop(0, tt, issue, 0, unroll=DMA_ISSUE_UNROLL)
    for k in range(TOP_K):
        pltpu.make_async_copy(ys_hbm.at[pl.ds(0, tt)], buf.at[k], sem).wait()
    w = wts_ref[...]
    w0, w1 = w[:, 0:1], w[:, 1:2]
    half = pack_cols // 2
    for j in range(x_ref.shape[1] // pack_cols):
        words = slice(j * half, (j + 1) * half)
        lo0, hi0 = _unpack_bf16_pairs(buf[0, :, words])
        lo1, hi1 = _unpack_bf16_pairs(buf[1, :, words])
        for y, c0 in ((w0 * lo0 + w1 * lo1, j * pack_cols), (w0 * hi0 + w1 * hi1, j * pack_cols + half)):
            cols = slice(c0, c0 + half)
            o_ref[:, cols] = x_ref[:, cols] + mod_ref[5:6, cols] * y


def _combine(x, wts, mod_l, row_map, ys, slots, pack_cols):
    n, d = x.shape
    tt = TOKEN_TILE
    return pl.pallas_call(
        functools.partial(_combine_kernel, pack_cols=pack_cols),
        grid_spec=pltpu.PrefetchScalarGridSpec(
            num_scalar_prefetch=1,
            grid=(n // tt,),
            in_specs=[
                pl.BlockSpec((tt, d), lambda i, s: (i, 0)),
                pl.BlockSpec((tt, LANES), lambda i, s: (i, 0)),
                pl.BlockSpec((None, 6, d), lambda i, s: (row_map(i), 0, 0)),
                pl.BlockSpec(memory_space=pl.ANY),
            ],
            out_specs=pl.BlockSpec((tt, d), lambda i, s: (i, 0)),
            scratch_shapes=[pltpu.VMEM((TOP_K, tt, d // 2), jnp.uint32), pltpu.SemaphoreType.DMA(())],
        ),
        out_shape=jax.ShapeDtypeStruct((n, d), F32),
        compiler_params=_cparams(("arbitrary",)),
        name="moe_combine",
    )(slots, x, wts, mod_l, ys)


def _rope_tables(seq):
    rows = seq // GRID_W
    row = jnp.repeat(jnp.arange(rows), GRID_W).astype(F32)
    col = jnp.tile(jnp.arange(GRID_W), rows).astype(F32)
    n_freq = HEAD_DIM // 4
    inv = 1.0 / (ROPE_BASE ** (jnp.arange(n_freq, dtype=F32) / n_freq))
    ang = jnp.concatenate([row[:, None] * inv, col[:, None] * inv], axis=-1)
    cos, sin = jnp.cos(ang), jnp.sin(ang)
    return jnp.concatenate([cos, cos], axis=-1), jnp.concatenate([-sin, sin], axis=-1)


def kernel(x_prompt, x_sample, c, cache_win_k, cache_win_v, cache_diff_k, cache_diff_v, state_rnn, c_ctx, w_mod, b_mod, norm1_g, norm2_g, final_g, w_in, conv_w, conv_b, rg_wa, rg_ba, rg_wx, rg_bx, rg_lambda, win_sink, diff_lambda, diff_subln_g, w_branch, w_out, moe_w_group, moe_b_group, moe_w_router, moe_b_router, moe_w_gate, moe_w_up, moe_w_down):
    bp, tp, d = x_prompt.shape
    bs, ts, _ = x_sample.shape
    depth = w_in.shape[0]
    past = cache_win_k.shape[2]
    n_kv = cache_win_k.shape[3]
    n_win = win_sink.shape[1]
    group = n_win // n_kv
    n_diff = cache_diff_k.shape[3]
    d_rnn = rg_ba.shape[2]
    n_groups = moe_w_group.shape[2]
    n_experts = moe_w_router.shape[2]
    per_group = n_experts // n_groups
    n_p, n_s = bp * tp, bs * ts
    tt = TOKEN_TILE
    assert tp % tt == 0 and ts % tt == 0 and n_p % ts == 0 and d_rnn == d
    assert n_groups + n_experts <= LANES

    sections = (d_rnn, d_rnn, n_win * HEAD_DIM, n_kv * HEAD_DIM, n_kv * HEAD_DIM,
                n_diff * 2 * HEAD_DIM, n_diff * 2 * HEAD_DIM, n_diff * 2 * HEAD_DIM, 3 * d)
    offs = [0]
    for s_ in sections:
        offs.append(offs[-1] + s_)
    _, _, o_wq, o_wk, o_wv, o_dq, o_dk, o_dv, o_mg = offs[:9]

    row_map = _mod_row_map(n_p // tt, ts // tt)

    n_rows = -(-(1 + bs) // SUBLANES) * SUBLANES
    cvecs = jnp.zeros((n_rows, d), F32).at[0].set(c_ctx).at[1:1 + bs].set(c)
    mod = _ada_mod_all(cvecs, w_mod, b_mod).reshape(depth, n_rows, 6, d)

    cos_f, sin_f = _rope_tables(ts)
    ck_win = cache_win_k.reshape(bs, depth, past, n_kv * HEAD_DIM)
    cv_win = cache_win_v.reshape(bs, depth, past, n_kv * HEAD_DIM)
    ck_diff = cache_diff_k.reshape(bs, depth, past, n_diff * 2 * HEAD_DIM)
    cv_diff = cache_diff_v.reshape(bs, depth, past, n_diff * 2 * HEAD_DIM)
    w_rt = jnp.zeros((depth, d, LANES), F32).at[:, :, :n_groups].set(moe_w_group) \
        .at[:, :, n_groups:n_groups + n_experts].set(moe_w_router)
    b_rt = jnp.zeros((depth, 1, LANES), F32).at[:, 0, :n_groups].set(moe_b_group) \
        .at[:, 0, n_groups:n_groups + n_experts].set(moe_b_router)
    w_rt_hi = w_rt.astype(BF16)
    w_rt_lo = (w_rt - w_rt_hi.astype(F32)).astype(BF16)
    h0_prompt = jnp.zeros((bp, 2, d_rnn), F32)
    norm1_g = norm1_g.reshape(depth, 1, d)
    norm2_g = norm2_g.reshape(depth, 1, d)
    conv_b = conv_b.reshape(depth, 1, d_rnn)
    diff_subln_g = diff_subln_g.reshape(depth, 1, 2 * HEAD_DIM)

    x = jnp.concatenate([x_prompt.reshape(n_p, d), x_sample.reshape(n_s, d)], axis=0)
    new_wk, new_wv, new_dk, new_dv, new_st = [], [], [], [], []
    for l in range(depth):
        lam_init = 0.8 - 0.6 * math.exp(-0.3 * l)
        mod_l = mod[l]
        hn = _norm1(x, norm1_g, l, mod_l, row_map)
        proj = _in_proj(hn, w_in, l)

        rg_args = (conv_w, conv_b, rg_wa, rg_wx, rg_ba, rg_bx, rg_lambda)
        oa_p, st_p = _rglru(proj, 0, bp, tp, d_rnn, *rg_args, h0_prompt, l)
        oa_s, _ = _rglru(proj, n_p // ts, bs, ts, d_rnn, *rg_args, state_rnn[:, l], l)

        ob_p = _win_ctx(proj, bp, tp, o_wq, o_wk, o_wv, n_kv, group, win_sink, l)
        ob_s = _win_lat(proj, n_p, bs, ts, o_wq, o_wk, o_wv, n_kv, group, win_sink, l,
                        ck_win, cv_win, cos_f, sin_f)
        oc_p = _diff_ctx(proj, bp, tp, o_dq, o_dk, o_dv, n_diff, diff_lambda, diff_subln_g, l, lam_init)
        oc_s = _diff_lat(proj, n_p, bs, ts, o_dq, o_dk, o_dv, n_diff, diff_lambda, diff_subln_g, l,
                         lam_init, ck_diff, cv_diff, cos_f, sin_f)

        merged = _merge((oa_p, ob_p, oc_p), (oa_s, ob_s, oc_s), proj, o_mg, w_branch, l)
        x = _out_proj(merged, w_out, l, x, mod_l, row_map)

        hn2, eid, wts, cnt = _norm2_router(x, norm2_g, l, mod_l, row_map, w_rt_hi[l], w_rt_lo[l],
                                           b_rt[l], n_groups, per_group)
        counts = cnt[0, n_groups:n_groups + n_experts].astype(jnp.int32)
        slot_tok, dest, blk_e, first, n_used = _moe_plan(
            eid[:, :TOP_K], eid[:, TOP_K:2 * TOP_K], counts, n_experts, MOE_ROWS)
        xs = _gather_rows(hn2, slot_tok)
        hmid = _gmm1(xs, moe_w_gate, moe_w_up, l, blk_e, first, n_used)
        ys, pack_cols = _gmm2(hmid, moe_w_down, l, blk_e, first, n_used)
        x = _combine(x, wts, mod_l, row_map, ys, dest, pack_cols)

        pr = proj[:n_p]
        new_wk.append(pr[:, o_wk:o_wv].reshape(bp, tp, n_kv, HEAD_DIM))
        new_wv.append(pr[:, o_wv:o_dq].reshape(bp, tp, n_kv, HEAD_DIM))
        new_dk.append(pr[:, o_dk:o_dv].reshape(bp, tp, n_diff, 2, HEAD_DIM))
        new_dv.append(pr[:, o_dv:o_mg].reshape(bp, tp, n_diff, 2 * HEAD_DIM))
        new_st.append(st_p)

    y = _final_norm(x, final_g)
    return (y[:n_p].reshape(bp, tp, d), y[n_p:].reshape(bs, ts, d),
            jnp.stack(new_wk, axis=1), jnp.stack(new_wv, axis=1),
            jnp.stack(new_dk, axis=1), jnp.stack(new_dv, axis=1), jnp.stack(new_st, axis=1))
```

```python
import functools
import math

import jax
import jax.numpy as jnp
from jax import lax
from jax.experimental import pallas as pl
from jax.experimental.pallas import tpu as pltpu

F32 = jnp.float32
BF16 = jnp.bfloat16

HEAD_DIM = 128
GRID_W = 64
ROPE_BASE = 10000.0
WINDOW = 128
CONV_LEFT = 2
RG_C = 8.0
TOP_K = 2
EPS = 1e-6
NEG_INF = -1e30

V7X_VMEM_LIMIT_BYTES = 56 * 1024 * 1024
LANES = 128
SUBLANES = 8

TOKEN_TILE = 256
MOE_ROWS = 256
DMA_ISSUE_UNROLL = 8
DMA_PRIORITIES = 2


def _cparams(sem):
    return pltpu.CompilerParams(dimension_semantics=sem, vmem_limit_bytes=V7X_VMEM_LIMIT_BYTES)


def _sigmoid(x):
    return 0.5 * jnp.tanh(0.5 * x) + 0.5


def _roll8(x, shift):
    r, c = x.shape
    return pltpu.roll(x.reshape(r // SUBLANES, SUBLANES, c), shift, 1).reshape(r, c)


def _pack_bf16_pairs(x):
    w = x.shape[1] // 2
    lo = lax.bitcast_convert_type(x[:, :w].astype(BF16).astype(F32), jnp.uint32) >> 16
    hi = lax.bitcast_convert_type(x[:, w:].astype(BF16).astype(F32), jnp.uint32) & jnp.uint32(0xFFFF0000)
    return hi | lo


def _unpack_bf16_pairs(p):
    lo = lax.bitcast_convert_type(p << 16, F32)
    hi = lax.bitcast_convert_type(p & jnp.uint32(0xFFFF0000), F32)
    return lo, hi


def _gelu_tanh(x):
    c = math.sqrt(2.0 / math.pi)
    return 0.5 * x * (1.0 + jnp.tanh(c * (x + 0.044715 * (x * x * x))))


def _largest_tile(n, candidates):
    for c in candidates:
        if n % c == 0:
            return c
    raise ValueError(f"no tile in {candidates} divides {n}")


def _cast_rows(src_ref, dst_ref, chunk):
    rows = src_ref.shape[0]
    chunk = min(chunk, rows)

    def body(k, c):
        r0 = pl.multiple_of(k * chunk, chunk)
        dst_ref[pl.ds(r0, chunk), :] = src_ref[pl.ds(r0, chunk), :].astype(dst_ref.dtype)
        return c

    lax.fori_loop(0, rows // chunk, body, 0)


def _adamod_kernel(cv_ref, w_ref, b_ref, o_ref):
    cv = cv_ref[...]
    s = cv * _sigmoid(cv)
    o_ref[...] = jnp.dot(s.astype(BF16), w_ref[...].astype(BF16),
                         preferred_element_type=F32) + b_ref[...]


def _ada_mod_all(cvecs, w_mod, b_mod):
    depth, d, n6 = w_mod.shape
    rows = cvecs.shape[0]
    tn = _largest_tile(n6, (512, 256, 128))
    return pl.pallas_call(
        _adamod_kernel,
        grid=(depth, n6 // tn),
        in_specs=[
            pl.BlockSpec((rows, d), lambda l, j: (0, 0)),
            pl.BlockSpec((None, d, tn), lambda l, j: (l, 0, j)),
            pl.BlockSpec((None, 1, tn), lambda l, j: (l, 0, j)),
        ],
        out_specs=pl.BlockSpec((None, rows, tn), lambda l, j: (l, 0, j)),
        out_shape=jax.ShapeDtypeStruct((depth, rows, n6), F32),
        compiler_params=_cparams(("arbitrary", "arbitrary")),
        name="ada_mod",
    )(cvecs, w_mod, b_mod.reshape(depth, 1, n6))


def _mod_row_map(n_prompt_tiles, tiles_per_latent_seq):
    def row(i):
        return jnp.where(i < n_prompt_tiles, 0, 1 + (i - n_prompt_tiles) // tiles_per_latent_seq)
    return row


def _norm_mod(x, g, mod_ref, shift_row, scale_row):
    ms = jnp.mean(x * x, axis=-1, keepdims=True)
    y = x * lax.rsqrt(ms + EPS) * g
    return y * (1.0 + mod_ref[scale_row:scale_row + 1, :]) + mod_ref[shift_row:shift_row + 1, :]


def _norm1_kernel(x_ref, g_ref, mod_ref, o_ref):
    o_ref[...] = _norm_mod(x_ref[...], g_ref[...], mod_ref, 0, 1).astype(o_ref.dtype)


def _norm1(x, g_all, layer, mod_l, row_map):
    n, d = x.shape
    tt = TOKEN_TILE
    return pl.pallas_call(
        _norm1_kernel,
        grid=(n // tt,),
        in_specs=[
            pl.BlockSpec((tt, d), lambda i: (i, 0)),
            pl.BlockSpec((None, 1, d), lambda i: (layer, 0, 0)),
            pl.BlockSpec((None, 6, d), lambda i: (row_map(i), 0, 0)),
        ],
        out_specs=pl.BlockSpec((tt, d), lambda i: (i, 0)),
        out_shape=jax.ShapeDtypeStruct((n, d), BF16),
        compiler_params=_cparams(("arbitrary",)),
        name="norm1_mod",
    )(x, g_all, mod_l)


def _norm2_router_kernel(x_ref, g_ref, mod_ref, wh_ref, wl_ref, br_ref, hn_ref, eid_ref, wts_ref,
                         cnt_ref, *, n_groups, per_group):
    y = _norm_mod(x_ref[...], g_ref[...], mod_ref, 3, 4)
    hn_ref[...] = _pack_bf16_pairs(y)
    y_hi = y.astype(BF16)
    y_lo = (y - y_hi.astype(F32)).astype(BF16)
    w_hi = wh_ref[...]
    logits = (jnp.dot(y_hi, w_hi, preferred_element_type=F32)
              + (jnp.dot(y_lo, w_hi, preferred_element_type=F32)
                 + jnp.dot(y_hi, wl_ref[...], preferred_element_type=F32))) + br_ref[...]
    lane = lax.broadcasted_iota(jnp.int32, logits.shape, 1).astype(F32)
    big = float(LANES)
    gmask = lane < n_groups
    glog = jnp.where(gmask, logits, NEG_INF)
    gmax = jnp.max(glog, axis=-1, keepdims=True)
    gsum = jnp.sum(jnp.where(gmask, jnp.exp(glog - gmax), 0.0), axis=-1, keepdims=True)
    g_p = 1.0 / gsum
    g_i = jnp.min(jnp.where(gmask & (glog == gmax), lane, big), axis=-1, keepdims=True)
    lo = n_groups + g_i * per_group
    rmask = (lane >= lo) & (lane < lo + per_group)
    rlog = jnp.where(rmask, logits, NEG_INF)
    m1 = jnp.max(rlog, axis=-1, keepdims=True)
    i1 = jnp.min(jnp.where(rmask & (rlog == m1), lane, big), axis=-1, keepdims=True)
    rmask2 = rmask & (lane != i1)
    rlog2 = jnp.where(rmask2, logits, NEG_INF)
    m2 = jnp.max(rlog2, axis=-1, keepdims=True)
    i2 = jnp.min(jnp.where(rmask2 & (rlog2 == m2), lane, big), axis=-1, keepdims=True)
    e = jnp.exp(m2 - m1)
    w1 = g_p / (1.0 + e)
    w2 = w1 * e
    wts_ref[...] = jnp.where(lane == 0, w1, jnp.where(lane == 1, w2, 0.0))
    tt = logits.shape[0]

    @pl.when(pl.program_id(0) == 0)
    def _():
        cnt_ref[...] = jnp.zeros(cnt_ref.shape, F32)

    onehot = jnp.where((lane == i1) | (lane == i2), 1.0, 0.0)
    earlier = (lax.broadcasted_iota(jnp.int32, (tt, tt), 1)
               < lax.broadcasted_iota(jnp.int32, (tt, tt), 0))
    prefix = jnp.dot(jnp.where(earlier, 1.0, 0.0).astype(BF16), onehot.astype(BF16),
                     preferred_element_type=F32)
    before = cnt_ref[...] + prefix
    rank1 = jnp.sum(jnp.where(lane == i1, before, 0.0), axis=-1, keepdims=True)
    rank2 = jnp.sum(jnp.where(lane == i2, before, 0.0), axis=-1, keepdims=True)
    cnt_ref[...] += jnp.sum(onehot, axis=0, keepdims=True)
    eid_ref[...] = jnp.where(lane == 0, i1 - n_groups,
                             jnp.where(lane == 1, i2 - n_groups,
                                       jnp.where(lane == 2, rank1,
                                                 jnp.where(lane == 3, rank2, 0.0)))).astype(jnp.int32)


def _norm2_router(x, g_all, layer, mod_l, row_map, w_rt_hi, w_rt_lo, b_rt, n_groups, per_group):
    n, d = x.shape
    tt = TOKEN_TILE
    kern = functools.partial(_norm2_router_kernel, n_groups=n_groups, per_group=per_group)
    return pl.pallas_call(
        kern,
        grid=(n // tt,),
        in_specs=[
            pl.BlockSpec((tt, d), lambda i: (i, 0)),
            pl.BlockSpec((None, 1, d), lambda i: (layer, 0, 0)),
            pl.BlockSpec((None, 6, d), lambda i: (row_map(i), 0, 0)),
            pl.BlockSpec((d, LANES), lambda i: (0, 0)),
            pl.BlockSpec((d, LANES), lambda i: (0, 0)),
            pl.BlockSpec((1, LANES), lambda i: (0, 0)),
        ],
        out_specs=[
            pl.BlockSpec((tt, d // 2), lambda i: (i, 0)),
            pl.BlockSpec((tt, LANES), lambda i: (i, 0)),
            pl.BlockSpec((tt, LANES), lambda i: (i, 0)),
            pl.BlockSpec((1, LANES), lambda i: (0, 0)),
        ],
        out_shape=[
            jax.ShapeDtypeStruct((n, d // 2), jnp.uint32),
            jax.ShapeDtypeStruct((n, LANES), jnp.int32),
            jax.ShapeDtypeStruct((n, LANES), F32),
            jax.ShapeDtypeStruct((1, LANES), F32),
        ],
        compiler_params=_cparams(("arbitrary",)),
        name="norm2_router",
    )(x, g_all, mod_l, w_rt_hi, w_rt_lo, b_rt)


def _final_norm_kernel(x_ref, g_ref, o_ref):
    x = x_ref[...]
    ms = jnp.mean(x * x, axis=-1, keepdims=True)
    o_ref[...] = x * lax.rsqrt(ms + EPS) * g_ref[...]


def _final_norm(x, g):
    n, d = x.shape
    tt = TOKEN_TILE
    return pl.pallas_call(
        _final_norm_kernel,
        grid=(n // tt,),
        in_specs=[pl.BlockSpec((tt, d), lambda i: (i, 0)),
                  pl.BlockSpec((1, d), lambda i: (0, 0))],
        out_specs=pl.BlockSpec((tt, d), lambda i: (i, 0)),
        out_shape=jax.ShapeDtypeStruct((n, d), F32),
        compiler_params=_cparams(("arbitrary",)),
        name="final_norm",
    )(x, g.reshape(1, d))


def _mm_kernel(x_ref, w_ref, o_ref, wbf_ref):
    @pl.when(pl.program_id(1) == 0)
    def _():
        _cast_rows(w_ref, wbf_ref, 512)

    o_ref[...] = jnp.dot(x_ref[...], wbf_ref[...], preferred_element_type=F32).astype(o_ref.dtype)


def _in_proj(hn, w_in_all, layer):
    m, k = hn.shape
    n = w_in_all.shape[2]
    tn = _largest_tile(n, (1024, 512, 256))
    tm = _largest_tile(m, (1024, 512, 256))
    return pl.pallas_call(
        _mm_kernel,
        grid=(n // tn, m // tm),
        in_specs=[
            pl.BlockSpec((tm, k), lambda j, i: (i, 0)),
            pl.BlockSpec((None, k, tn), lambda j, i: (layer, 0, j), pipeline_mode=pl.Buffered(1)),
        ],
        out_specs=pl.BlockSpec((tm, tn), lambda j, i: (i, j)),
        out_shape=jax.ShapeDtypeStruct((m, n), F32),
        scratch_shapes=[pltpu.VMEM((k, tn), BF16)],
        compiler_params=_cparams(("arbitrary", "arbitrary")),
        name="in_proj",
    )(hn, w_in_all)


MERGE_W_CHUNK = 256
MERGE_K_SPLIT = 2


def _merge_kernel(oap_ref, obp_ref, ocp_ref, oas_ref, obs_ref, ocs_ref, ga_ref, gb_ref, gc_ref, w_hbm,
                  o_ref, wbf, stage, sem, acc, *, layer, n_ctx_tiles):
    dk = oap_ref.shape[1]
    d = wbf.shape[0] // 3
    tn = o_ref.shape[1]
    ck = stage.shape[1]
    n_ck = wbf.shape[0] // ck
    kk = pl.program_id(2)

    @pl.when((pl.program_id(1) == 0) & (kk == 0))
    def _():
        col0 = pl.multiple_of(pl.program_id(0) * tn, tn)

        def copy(c, slot):
            r0 = pl.multiple_of(c * ck, ck)
            return pltpu.make_async_copy(w_hbm.at[layer, pl.ds(r0, ck), pl.ds(col0, tn)],
                                         stage.at[slot], sem.at[slot])

        copy(0, 0).start()

        def body(c, carry):
            slot = c & 1

            @pl.when(c + 1 < n_ck)
            def _():
                copy(c + 1, 1 - slot).start()

            copy(c, slot).wait()
            r0 = pl.multiple_of(c * ck, ck)
            wbf[pl.ds(r0, ck), :] = stage[slot].astype(BF16)
            return carry

        lax.fori_loop(0, n_ck, body, 0)

    def part(o_ref_, r):
        k0 = pl.multiple_of(r * d + kk * dk, dk)
        return jnp.dot(o_ref_[...], wbf[pl.ds(k0, dk), :], preferred_element_type=F32)

    def gate(g_ref_):
        return 0.5 * jnp.tanh(0.5 * g_ref_[...]) + 0.5

    def merge(oa_ref, ob_ref, oc_ref):
        branches = ((oa_ref, ga_ref), (ob_ref, gb_ref), (oc_ref, gc_ref))

        @pl.when(kk == 0)
        def _():
            for r, (x_ref, _) in enumerate(branches):
                acc[r] = part(x_ref, r)

        @pl.when((kk > 0) & (kk < MERGE_K_SPLIT - 1))
        def _():
            for r, (x_ref, _) in enumerate(branches):
                acc[r] += part(x_ref, r)

        @pl.when(kk == MERGE_K_SPLIT - 1)
        def _():
            out = None
            for r, (x_ref, g_ref_) in enumerate(branches):
                term = gate(g_ref_) * (acc[r] + part(x_ref, r))
                out = term if out is None else out + term
            o_ref[...] = out.astype(o_ref.dtype)

    @pl.when(pl.program_id(1) < n_ctx_tiles)
    def _():
        merge(oap_ref, obp_ref, ocp_ref)

    @pl.when(pl.program_id(1) >= n_ctx_tiles)
    def _():
        merge(oas_ref, obs_ref, ocs_ref)


def _merge(mix_ctx, mix_lat, proj, gate_off, w_branch_all, layer):
    n_p, d = mix_ctx[0].shape
    m = n_p + mix_lat[0].shape[0]
    n = w_branch_all.shape[2]
    tn = _largest_tile(math.gcd(n, gate_off), (1024, 512, 256))
    tm = TOKEN_TILE
    ck = MERGE_W_CHUNK
    nk = MERGE_K_SPLIT
    dk = d // nk
    assert nk >= 2 and d % nk == 0 and dk % ck == 0 and n_p % tm == 0
    gb = gate_off // tn
    nb = n // tn
    npt = n_p // tm
    cspec = pl.BlockSpec((tm, dk), lambda j, i, k: (jnp.minimum(i, npt - 1),
                                                    jnp.where(i < npt, k, nk - 1)))
    lspec = pl.BlockSpec((tm, dk), lambda j, i, k: (jnp.maximum(i - npt, 0),
                                                    jnp.where(i >= npt, k, 0)))
    gspec = lambda r: pl.BlockSpec((tm, tn), lambda j, i, k: (i, gb + r * nb + j))
    return pl.pallas_call(
        functools.partial(_merge_kernel, layer=layer, n_ctx_tiles=npt),
        grid=(nb, m // tm, nk),
        in_specs=[cspec, cspec, cspec, lspec, lspec, lspec, gspec(0), gspec(1), gspec(2),
                  pl.BlockSpec(memory_space=pl.ANY)],
        out_specs=pl.BlockSpec((tm, tn), lambda j, i, k: (i, j)),
        out_shape=jax.ShapeDtypeStruct((m, n), BF16),
        scratch_shapes=[pltpu.VMEM((3 * d, tn), BF16), pltpu.VMEM((2, ck, tn), F32),
                        pltpu.SemaphoreType.DMA((2,)), pltpu.VMEM((3, tm, tn), F32)],
        compiler_params=_cparams(("arbitrary", "arbitrary", "arbitrary")),
        name="branch_merge",
    )(*mix_ctx, *mix_lat, proj, proj, proj, w_branch_all)


def _out_proj_kernel(h_ref, w_ref, x_ref, mod_ref, o_ref, wbf_ref):
    @pl.when(pl.program_id(1) == 0)
    def _():
        _cast_rows(w_ref, wbf_ref, 512)

    mix = jnp.dot(h_ref[...], wbf_ref[...], preferred_element_type=F32)
    o_ref[...] = x_ref[...] + mod_ref[2:3, :] * mix


def _out_proj(merged, w_out_all, layer, x, mod_l, row_map):
    m, k = merged.shape
    n = w_out_all.shape[2]
    tn = _largest_tile(n, (1024, 512, 256))
    tm = TOKEN_TILE
    return pl.pallas_call(
        _out_proj_kernel,
        grid=(n // tn, m // tm),
        in_specs=[
            pl.BlockSpec((tm, k), lambda j, i: (i, 0)),
            pl.BlockSpec((None, k, tn), lambda j, i: (layer, 0, j), pipeline_mode=pl.Buffered(1)),
            pl.BlockSpec((tm, tn), lambda j, i: (i, j)),
            pl.BlockSpec((None, 6, tn), lambda j, i: (row_map(i), 0, j)),
        ],
        out_specs=pl.BlockSpec((tm, tn), lambda j, i: (i, j)),
        out_shape=jax.ShapeDtypeStruct((m, n), F32),
        scratch_shapes=[pltpu.VMEM((k, tn), BF16)],
        compiler_params=_cparams(("arbitrary", "arbitrary")),
        name="out_proj",
    )(merged, w_out_all, x, mod_l)


RG_CHUNK = 128


def _rglru_kernel(rx_ref, rg_ref, cw_ref, cb_ref, waf_ref, wxf_ref, wab_ref, wxb_ref,
                  ba_ref, bx_ref, lam_ref, h0_ref, o_ref, st_ref,
                  af, uf, ab, ub, wbf, *, seq):
    tc = RG_CHUNK
    n_chunks = seq // tc
    c = rx_ref.shape[1]
    wbf[0] = waf_ref[...].astype(BF16)
    wbf[1] = wxf_ref[...].astype(BF16)
    wbf[2] = wab_ref[...].astype(BF16)
    wbf[3] = wxb_ref[...].astype(BF16)
    cw = cw_ref[...]
    cb = cb_ref[...]

    def softplus(z):
        return jnp.maximum(z, 0.0) + jnp.log1p(jnp.exp(-jnp.abs(z)))

    sp_f = softplus(-lam_ref[0:1, :])
    sp_b = softplus(-lam_ref[1:2, :])
    row8 = lax.broadcasted_iota(jnp.int32, (tc, c), 0) & (SUBLANES - 1)

    def prep(k, carry):
        r0 = pl.multiple_of(k * tc, tc)
        cur = rx_ref[pl.ds(r0, tc), :]
        p0 = pl.multiple_of(jnp.maximum(r0 - SUBLANES, 0), SUBLANES)
        n0 = pl.multiple_of(jnp.minimum(r0 + tc, seq - SUBLANES), SUBLANES)
        prev8 = jnp.where(k > 0, rx_ref[pl.ds(p0, SUBLANES), :], 0.0)
        next8 = jnp.where(k < n_chunks - 1, rx_ref[pl.ds(n0, SUBLANES), :], 0.0)
        ext = jnp.concatenate([prev8, cur, next8], axis=0)
        ne = tc + 2 * SUBLANES
        xm2 = pltpu.roll(ext, 2, 0)[SUBLANES:SUBLANES + tc]
        xm1 = pltpu.roll(ext, 1, 0)[SUBLANES:SUBLANES + tc]
        xp1 = pltpu.roll(ext, ne - 1, 0)[SUBLANES:SUBLANES + tc]
        xc = cb + xm2 * cw[0:1] + xm1 * cw[1:2] + cur * cw[2:3] + xp1 * cw[3:4]
        xcb = xc.astype(BF16)

        def gates(wi, d, sp):
            r = _sigmoid(jnp.dot(xcb, wbf[wi], preferred_element_type=F32) + ba_ref[d:d + 1, :])
            i = _sigmoid(jnp.dot(xcb, wbf[wi + 1], preferred_element_type=F32) + bx_ref[d:d + 1, :])
            log_a = (-RG_C) * r * sp
            a = jnp.exp(log_a)
            u = jnp.sqrt(1.0 - a * a) * i * xc
            return a, u

        a, u = gates(0, 0, sp_f)
        for dd in (1, 2, 4):
            msk = row8 >= dd
            a_s = jnp.where(msk, _roll8(a, dd), 1.0)
            u_s = jnp.where(msk, _roll8(u, dd), 0.0)
            u = a * u_s + u
            a = a * a_s
        af[pl.ds(r0, tc), :] = a
        uf[pl.ds(r0, tc), :] = u

        a, u = gates(2, 1, sp_b)
        for dd in (1, 2, 4):
            msk = row8 < SUBLANES - dd
            a_s = jnp.where(msk, _roll8(a, SUBLANES - dd), 1.0)
            u_s = jnp.where(msk, _roll8(u, SUBLANES - dd), 0.0)
            u = a * u_s + u
            a = a * a_s
        ab[pl.ds(r0, tc), :] = a
        ub[pl.ds(r0, tc), :] = u
        return carry

    lax.fori_loop(0, n_chunks, prep, 0)

    n8 = seq // SUBLANES

    def carry_step(k, carry):
        hf, hb = carry
        i0 = pl.multiple_of(k * SUBLANES, SUBLANES)
        hf_t = af[pl.ds(i0, SUBLANES), :] * hf + uf[pl.ds(i0, SUBLANES), :]
        uf[pl.ds(i0, SUBLANES), :] = hf_t
        j0 = pl.multiple_of((n8 - 1 - k) * SUBLANES, SUBLANES)
        hb_t = ab[pl.ds(j0, SUBLANES), :] * hb + ub[pl.ds(j0, SUBLANES), :]
        ub[pl.ds(j0, SUBLANES), :] = hb_t
        return hf_t[SUBLANES - 1:SUBLANES, :], hb_t[0:1, :]

    hf, hb = lax.fori_loop(0, n8, carry_step, (h0_ref[0:1, :], h0_ref[1:2, :]))
    st_ref[0:1, :] = hf
    st_ref[1:2, :] = hb

    def finish(k, carry):
        r0 = pl.multiple_of(k * tc, tc)
        hsum = uf[pl.ds(r0, tc), :] + ub[pl.ds(r0, tc), :]
        o_ref[pl.ds(r0, tc), :] = (hsum * _gelu_tanh(rg_ref[pl.ds(r0, tc), :])).astype(o_ref.dtype)
        return carry

    lax.fori_loop(0, n_chunks, finish, 0)


def _rglru(proj, row_blk0, n_seq, seq, d, conv_w, conv_b, rg_wa, rg_wx, rg_ba, rg_bx, rg_lambda,
           h0, layer):
    nb, c = rg_wa.shape[2], rg_wa.shape[3]
    kern = functools.partial(_rglru_kernel, seq=seq)
    wspec = lambda direction: pl.BlockSpec((None, None, None, c, c),
                                           lambda b, n: (layer, direction, n, 0, 0))
    vec2 = pl.BlockSpec((None, 2, c), lambda b, n: (layer, 0, n))
    return pl.pallas_call(
        kern,
        grid=(n_seq, nb),
        in_specs=[
            pl.BlockSpec((seq, c), lambda b, n: (row_blk0 + b, n)),
            pl.BlockSpec((seq, c), lambda b, n: (row_blk0 + b, nb + n)),
            pl.BlockSpec((None, conv_w.shape[1], c), lambda b, n: (layer, 0, n)),
            pl.BlockSpec((None, 1, c), lambda b, n: (layer, 0, n)),
            wspec(0), wspec(0), wspec(1), wspec(1),
            vec2, vec2, vec2,
            pl.BlockSpec((None, 2, c), lambda b, n: (b, 0, n)),
        ],
        out_specs=[
            pl.BlockSpec((seq, c), lambda b, n: (b, n)),
            pl.BlockSpec((None, 2, c), lambda b, n: (b, 0, n)),
        ],
        out_shape=[
            jax.ShapeDtypeStruct((n_seq * seq, d), BF16),
            jax.ShapeDtypeStruct((n_seq, 2, d), F32),
        ],
        scratch_shapes=[pltpu.VMEM((seq, c), F32)] * 4 + [pltpu.VMEM((4, c, c), BF16)],
        compiler_params=_cparams(("arbitrary", "arbitrary")),
        name="rglru",
    )(proj, proj, conv_w, conv_b, rg_wa, rg_wx, rg_wa, rg_wx, rg_ba, rg_bx, rg_lambda, h0)


def _rope(x, cos_f, sin_f):
    return x * cos_f + pltpu.roll(x, HEAD_DIM // 2, 1) * sin_f


def _sink_softmax_pv(s, sink_col, v):
    m = jnp.maximum(jnp.max(s, axis=-1, keepdims=True), sink_col)
    p = jnp.exp(s - m)
    denom = jnp.sum(p, axis=-1, keepdims=True) + jnp.exp(sink_col - m)
    o = jnp.dot(p.astype(BF16), v, preferred_element_type=F32)
    return o / denom


def _sink_col(sink_ref, layer, hkv, group, rows_per_head):
    rows = group * rows_per_head
    assert rows_per_head & (rows_per_head - 1) == 0
    head = lax.broadcasted_iota(jnp.int32, (rows, 1), 0) >> (rows_per_head.bit_length() - 1)
    col = jnp.zeros((rows, 1), F32)
    for g in range(group):
        col = jnp.where(head == g, sink_ref[layer, hkv * group + g], col)
    return col


WIN_KV_PER_STEP = 2


def _kv_per_step(n_kv, *col_offsets, unit):
    kvs = WIN_KV_PER_STEP if n_kv % WIN_KV_PER_STEP == 0 else 1
    if any(off % (kvs * unit) for off in col_offsets):
        kvs = 1
    return kvs


def _win_ctx_kernel(sink_ref, q_ref, k_ref, v_ref, o_ref, *, layer, group, kvs):
    tq = q_ref.shape[0]
    hd = HEAD_DIM
    gw = group * hd
    for hh in range(kvs):
        hkv = pl.program_id(1) * kvs + hh
        q = jnp.concatenate([q_ref[:, hh * gw + g * hd:hh * gw + (g + 1) * hd] * ATTN_SCALE
                             for g in range(group)], axis=0)
        s = lax.dot_general(q.astype(BF16), k_ref[:, hh * hd:(hh + 1) * hd].astype(BF16), _DN_QKT,
                            preferred_element_type=F32)
        o = _sink_softmax_pv(s, _sink_col(sink_ref, layer, hkv, group, tq),
                             v_ref[:, hh * hd:(hh + 1) * hd].astype(BF16))
        for g in range(group):
            o_ref[:, hh * gw + g * hd:hh * gw + (g + 1) * hd] = o[g * tq:(g + 1) * tq].astype(o_ref.dtype)


def _win_ctx(proj, n_seq, seq, q_off, k_off, v_off, n_kv, group, win_sink, layer):
    gw = group * HEAD_DIM
    kvs = _kv_per_step(n_kv, k_off, v_off, unit=HEAD_DIM)
    kw = kvs * HEAD_DIM
    assert q_off % (kvs * gw) == 0
    kern = functools.partial(_win_ctx_kernel, layer=layer, group=group, kvs=kvs)
    return pl.pallas_call(
        kern,
        grid_spec=pltpu.PrefetchScalarGridSpec(
            num_scalar_prefetch=1,
            grid=(n_seq, n_kv // kvs),
            in_specs=[
                pl.BlockSpec((seq, kvs * gw), lambda b, h, s: (b, q_off // (kvs * gw) + h)),
                pl.BlockSpec((seq, kw), lambda b, h, s: (b, k_off // kw + h)),
                pl.BlockSpec((seq, kw), lambda b, h, s: (b, v_off // kw + h)),
            ],
            out_specs=pl.BlockSpec((seq, kvs * gw), lambda b, h, s: (b, h)),
        ),
        out_shape=jax.ShapeDtypeStruct((n_seq * seq, n_kv * gw), BF16),
        compiler_params=_cparams(("arbitrary", "arbitrary")),
        name="win_attn_ctx",
    )(win_sink, proj, proj, proj)


def _win_lat_kernel(sink_ref, q_ref, kp_ref, kc_ref, kn_ref, vp_ref, vc_ref, vn_ref, ck_ref, cv_ref,
                    cq_ref, sq_ref, cp_ref, sp_ref, cc_ref, sc_ref, cn_ref, sn_ref, o_ref, *,
                    layer, group, n_qblk, kvs):
    qi = pl.program_id(2)
    tq = q_ref.shape[0]
    past = ck_ref.shape[0]
    hd = HEAD_DIM
    gw = group * hd
    cq, sq = cq_ref[...], sq_ref[...]
    cp, sp, cc, sc, cn, sn = (cp_ref[...], sp_ref[...], cc_ref[...], sc_ref[...], cn_ref[...],
                              sn_ref[...])
    shape = (group * tq, past + 3 * tq)
    r = lax.broadcasted_iota(jnp.int32, shape, 0) & (tq - 1)
    c = lax.broadcasted_iota(jnp.int32, shape, 1) - past
    in_prev = (c >= 0) & (c < tq)
    in_next = c >= 2 * tq
    bad = (in_prev & ((c < r + (tq - WINDOW)) | (qi == 0))) | \
          (in_next & ((c - 2 * tq > r - (tq - WINDOW)) | (qi == n_qblk - 1)))
    for hh in range(kvs):
        hkv = pl.program_id(1) * kvs + hh
        kcol = slice(hh * hd, (hh + 1) * hd)
        q = jnp.concatenate(
            [_rope(q_ref[:, hh * gw + g * hd:hh * gw + (g + 1) * hd], cq, sq) * ATTN_SCALE
             for g in range(group)], axis=0).astype(BF16)
        k = jnp.concatenate([
            ck_ref[:, kcol].astype(BF16),
            _rope(kp_ref[:, kcol], cp, sp).astype(BF16),
            _rope(kc_ref[:, kcol], cc, sc).astype(BF16),
            _rope(kn_ref[:, kcol], cn, sn).astype(BF16)], axis=0)
        v = jnp.concatenate([cv_ref[:, kcol].astype(BF16), vp_ref[:, kcol].astype(BF16),
                             vc_ref[:, kcol].astype(BF16), vn_ref[:, kcol].astype(BF16)], axis=0)
        s = lax.dot_general(q, k, _DN_QKT, preferred_element_type=F32)
        s = jnp.where(bad, NEG_INF, s)
        o = _sink_softmax_pv(s, _sink_col(sink_ref, layer, hkv, group, tq), v)
        for g in range(group):
            o_ref[:, hh * gw + g * hd:hh * gw + (g + 1) * hd] = o[g * tq:(g + 1) * tq].astype(o_ref.dtype)


def _win_lat(proj, row0, n_seq, seq, q_off, k_off, v_off, n_kv, group, win_sink, layer,
             cache_k, cache_v, cos_f, sin_f):
    tq = WINDOW
    assert seq % tq == 0 and row0 % tq == 0
    nq = seq // tq
    gw = group * HEAD_DIM
    kvs = _kv_per_step(n_kv, k_off, v_off, unit=HEAD_DIM)
    kw = kvs * HEAD_DIM
    assert q_off % (kvs * gw) == 0
    past = cache_k.shape[2]
    rb0 = row0 // tq
    kb, vb = k_off // kw, v_off // kw
    kern = functools.partial(_win_lat_kernel, layer=layer, group=group, n_qblk=nq, kvs=kvs)
    prv = lambda i: jnp.maximum(i - 1, 0)
    nxt = lambda i: jnp.minimum(i + 1, nq - 1)
    kv = lambda colb, f: pl.BlockSpec((tq, kw), lambda b, h, i, s: (rb0 + b * nq + f(i), colb + h))
    cache = pl.BlockSpec((None, None, past, kw), lambda b, h, i, s: (b, layer, 0, h))
    tab = lambda f: pl.BlockSpec((tq, HEAD_DIM), lambda b, h, i, s: (f(i), 0))
    same = lambda i: i
    return pl.pallas_call(
        kern,
        grid_spec=pltpu.PrefetchScalarGridSpec(
            num_scalar_prefetch=1,
            grid=(n_seq, n_kv // kvs, nq),
            in_specs=[
                pl.BlockSpec((tq, kvs * gw),
                             lambda b, h, i, s: (rb0 + b * nq + i, q_off // (kvs * gw) + h)),
                kv(kb, prv), kv(kb, same), kv(kb, nxt),
                kv(vb, prv), kv(vb, same), kv(vb, nxt),
                cache, cache,
                tab(same), tab(same), tab(prv), tab(prv), tab(same), tab(same), tab(nxt), tab(nxt),
            ],
            out_specs=pl.BlockSpec((tq, kvs * gw), lambda b, h, i, s: (b * nq + i, h)),
        ),
        out_shape=jax.ShapeDtypeStruct((n_seq * seq, n_kv * gw), BF16),
        compiler_params=_cparams(("arbitrary", "arbitrary", "arbitrary")),
        name="win_attn_lat",
    )(win_sink, proj, proj, proj, proj, proj, proj, proj, cache_k, cache_v,
      cos_f, sin_f, cos_f, sin_f, cos_f, sin_f, cos_f, sin_f)


def _diff_lambda(l4, lam_init):
    t1 = jnp.sum(l4[0:1, :] * l4[1:2, :], axis=-1, keepdims=True)
    t2 = jnp.sum(l4[2:3, :] * l4[3:4, :], axis=-1, keepdims=True)
    return jnp.exp(t1) - jnp.exp(t2) + lam_init


ATTN_SCALE = HEAD_DIM ** -0.5
_DN_QKT = (((1,), (1,)), ((), ()))


def _subln(o, g, lam_init):
    ms = jnp.mean(o * o, axis=-1, keepdims=True)
    return o * lax.rsqrt(ms + EPS) * g * (1.0 - lam_init)


def _diff_core(q0, q1, k0, k1, v, lam, g, lam_init):
    s0 = lax.dot_general(q0, k0, _DN_QKT, preferred_element_type=F32)
    s1 = lax.dot_general(q1, k1, _DN_QKT, preferred_element_type=F32)
    e0 = jnp.exp(s0 - jnp.max(s0, axis=-1, keepdims=True))
    e1 = jnp.exp(s1 - jnp.max(s1, axis=-1, keepdims=True))
    c0 = 1.0 / jnp.sum(e0, axis=-1, keepdims=True)
    c1 = lam / jnp.sum(e1, axis=-1, keepdims=True)
    a = e0 * c0 - e1 * c1
    o = jnp.dot(a.astype(BF16), v, preferred_element_type=F32)
    return _subln(o, g, lam_init)


def _diff_ctx_kernel(q_ref, k_ref, v_ref, l4_ref, g_ref, o_ref, *, lam_init):
    lam = _diff_lambda(l4_ref[...], lam_init)
    hd = HEAD_DIM
    for h in range(q_ref.shape[1] // (2 * hd)):
        c0, c1, c2 = 2 * h * hd, (2 * h + 1) * hd, (2 * h + 2) * hd
        o = _diff_core((q_ref[:, c0:c1] * ATTN_SCALE).astype(BF16),
                       (q_ref[:, c1:c2] * ATTN_SCALE).astype(BF16),
                       k_ref[:, c0:c1].astype(BF16), k_ref[:, c1:c2].astype(BF16),
                       v_ref[:, c0:c2].astype(BF16), lam, g_ref[...], lam_init)
        o_ref[:, c0:c2] = o.astype(o_ref.dtype)


DIFF_CTX_HEADS_PER_STEP = 4


def _diff_ctx(proj, n_seq, seq, q_off, k_off, v_off, n_heads, diff_lambda, subln_g, layer, lam_init):
    hps = DIFF_CTX_HEADS_PER_STEP if n_heads % DIFF_CTX_HEADS_PER_STEP == 0 else 1
    hw = 2 * HEAD_DIM
    bw = hps * hw
    kern = functools.partial(_diff_ctx_kernel, lam_init=lam_init)
    blk = lambda off: pl.BlockSpec((seq, bw), lambda b, h: (b, off // bw + h))
    assert q_off % bw == 0 and k_off % bw == 0 and v_off % bw == 0
    return pl.pallas_call(
        kern,
        grid=(n_seq, n_heads // hps),
        in_specs=[blk(q_off), blk(k_off), blk(v_off),
                  pl.BlockSpec((None, 4, HEAD_DIM), lambda b, h: (layer, 0, 0)),
                  pl.BlockSpec((None, 1, hw), lambda b, h: (layer, 0, 0))],
        out_specs=pl.BlockSpec((seq, bw), lambda b, h: (b, h)),
        out_shape=jax.ShapeDtypeStruct((n_seq * seq, n_heads * hw), BF16),
        compiler_params=_cparams(("arbitrary", "arbitrary")),
        name="diff_attn_ctx",
    )(proj, proj, proj, diff_lambda, subln_g)


DIFF_KV_CHUNK = 256


def _diff_lat_kernel(q_ref, k_ref, v_ref, ck_ref, cv_ref, cq_ref, sq_ref, call_ref, sall_ref,
                     l4_ref, g_ref, o_ref, kall, vall, *, lam_init):
    hd = HEAD_DIM
    past = ck_ref.shape[0]
    seq = k_ref.shape[0]

    @pl.when(pl.program_id(2) == 0)
    def _():
        kall[0:past, :] = ck_ref[...].astype(BF16)
        vall[0:past, :] = cv_ref[...].astype(BF16)

        def fill(i, carry):
            r0 = pl.multiple_of(i * DIFF_KV_CHUNK, DIFF_KV_CHUNK)
            rows = pl.ds(r0, DIFF_KV_CHUNK)
            dst = pl.ds(past + r0, DIFF_KV_CHUNK)
            cs, sn = call_ref[rows, :], sall_ref[rows, :]
            kall[dst, 0:hd] = _rope(k_ref[rows, 0:hd], cs, sn).astype(BF16)
            kall[dst, hd:2 * hd] = _rope(k_ref[rows, hd:2 * hd], cs, sn).astype(BF16)
            vall[dst, :] = v_ref[rows, :].astype(BF16)
            return carry

        lax.fori_loop(0, seq // DIFF_KV_CHUNK, fill, 0)

    lam = _diff_lambda(l4_ref[...], lam_init)
    cq, sq = cq_ref[...], sq_ref[...]
    q0 = (_rope(q_ref[:, 0:hd], cq, sq) * ATTN_SCALE).astype(BF16)
    q1 = (_rope(q_ref[:, hd:2 * hd], cq, sq) * ATTN_SCALE).astype(BF16)
    o = _diff_core(q0, q1, kall[:, 0:hd], kall[:, hd:2 * hd], vall[...], lam, g_ref[...], lam_init)
    o_ref[...] = o.astype(o_ref.dtype)


def _diff_lat(proj, row0, n_seq, seq, q_off, k_off, v_off, n_heads, diff_lambda, subln_g, layer,
              lam_init, cache_k, cache_v, cos_f, sin_f):
    hw = 2 * HEAD_DIM
    tq = 256
    kc = DIFF_KV_CHUNK
    assert seq % kc == 0 and row0 % seq == 0 and seq % tq == 0
    nq = seq // tq
    past = cache_k.shape[2]
    sb0 = row0 // seq
    kern = functools.partial(_diff_lat_kernel, lam_init=lam_init)
    full = lambda off: pl.BlockSpec((seq, hw), lambda b, h, i: (sb0 + b, off // hw + h))
    cache = pl.BlockSpec((None, None, past, hw), lambda b, h, i: (b, layer, 0, h))
    return pl.pallas_call(
        kern,
        grid=(n_seq, n_heads, nq),
        in_specs=[
            pl.BlockSpec((tq, hw), lambda b, h, i: (sb0 * nq + b * nq + i, q_off // hw + h)),
            full(k_off), full(v_off), cache, cache,
            pl.BlockSpec((tq, HEAD_DIM), lambda b, h, i: (i, 0)),
            pl.BlockSpec((tq, HEAD_DIM), lambda b, h, i: (i, 0)),
            pl.BlockSpec((seq, HEAD_DIM), lambda b, h, i: (0, 0)),
            pl.BlockSpec((seq, HEAD_DIM), lambda b, h, i: (0, 0)),
            pl.BlockSpec((None, 4, HEAD_DIM), lambda b, h, i: (layer, 0, 0)),
            pl.BlockSpec((None, 1, hw), lambda b, h, i: (layer, 0, 0)),
        ],
        out_specs=pl.BlockSpec((tq, hw), lambda b, h, i: (b * nq + i, h)),
        out_shape=jax.ShapeDtypeStruct((n_seq * seq, n_heads * hw), BF16),
        scratch_shapes=[pltpu.VMEM((past + seq, hw), BF16), pltpu.VMEM((past + seq, hw), BF16)],
        compiler_params=_cparams(("arbitrary", "arbitrary", "arbitrary")),
        name="diff_attn_lat",
    )(proj, proj, proj, cache_k, cache_v, cos_f, sin_f, cos_f, sin_f, diff_lambda, subln_g)


def _moe_plan(eid, rank, counts, n_experts, rows_per_blk):
    n = eid.shape[0]
    a = n * TOP_K
    flat_e = eid.reshape(-1)
    nblk_e = (counts + rows_per_blk - 1) // rows_per_blk
    blk_end = jnp.cumsum(nblk_e)
    blk_start = blk_end - nblk_e
    is_e = flat_e[:, None] == jnp.arange(n_experts, dtype=jnp.int32)[None, :]
    start_of = jnp.sum(jnp.where(is_e, blk_start[None, :], 0), axis=1)
    dest = (start_of * rows_per_blk + rank.reshape(-1)).astype(jnp.int32)
    n_blk = a // rows_per_blk + n_experts
    n_used = blk_end[-1].astype(jnp.int32)
    slot_tok = jnp.zeros((n_blk * rows_per_blk,), jnp.int32).at[dest].set(
        jnp.arange(a, dtype=jnp.int32) // TOP_K)
    blk_ids = jnp.arange(n_blk, dtype=jnp.int32)
    blk_e = jnp.minimum(jnp.searchsorted(blk_end, blk_ids, side='right'), n_experts - 1)
    blk_e = jnp.where(blk_ids < n_used, blk_e, blk_e[jnp.maximum(n_used - 1, 0)]).astype(jnp.int32)
    first = jnp.concatenate([jnp.ones((1,), jnp.int32),
                             (blk_e[1:] != blk_e[:-1]).astype(jnp.int32)])
    return slot_tok, dest, blk_e, first, n_used.reshape(1)


def _gather_rows_kernel(tok_ref, x_hbm, o_ref, buf, sem, *, rows_per_step):
    i = pl.program_id(0)
    slot = i & 1

    def issue(step, dst_slot):
        base = step * rows_per_step

        def body(p, carry):
            for prio in range(DMA_PRIORITIES):
                r = DMA_PRIORITIES * p + prio
                tok = tok_ref[base + r]
                pltpu.make_async_copy(x_hbm.at[pl.ds(tok, 1)], buf.at[dst_slot, pl.ds(r, 1)],
                                      sem.at[dst_slot]).start(priority=prio)
            return carry

        lax.fori_loop(0, rows_per_step // DMA_PRIORITIES, body, 0,
                      unroll=DMA_ISSUE_UNROLL // DMA_PRIORITIES)

    @pl.when(i == 0)
    def _():
        issue(0, 0)

    @pl.when(i + 1 < pl.num_programs(0))
    def _():
        issue(i + 1, 1 - slot)

    pltpu.make_async_copy(x_hbm.at[pl.ds(0, rows_per_step)], buf.at[slot], sem.at[slot]).wait()
    w = buf.shape[2]
    lo, hi = _unpack_bf16_pairs(buf[slot])
    o_ref[:, 0:w] = lo.astype(o_ref.dtype)
    o_ref[:, w:2 * w] = hi.astype(o_ref.dtype)


def _gather_rows(x, slot_tok):
    rows = slot_tok.shape[0]
    w = x.shape[1]
    d = 2 * w
    rps = MOE_ROWS
    kern = functools.partial(_gather_rows_kernel, rows_per_step=rps)
    return pl.pallas_call(
        kern,
        grid_spec=pltpu.PrefetchScalarGridSpec(
            num_scalar_prefetch=1,
            grid=(rows // rps,),
            in_specs=[pl.BlockSpec(memory_space=pl.ANY)],
            out_specs=pl.BlockSpec((rps, d), lambda i, tok: (i, 0)),
            scratch_shapes=[pltpu.VMEM((2, rps, w), x.dtype), pltpu.SemaphoreType.DMA((2,))],
        ),
        out_shape=jax.ShapeDtypeStruct((rows, d), BF16),
        compiler_params=_cparams(("arbitrary",)),
        name="moe_gather",
    )(slot_tok, x)


def _gmm1_kernel(be_ref, first_ref, nused_ref, x_ref, wg_ref, wu_ref, o_ref, wgb, wub):
    blk = pl.program_id(1)

    @pl.when(first_ref[blk] == 1)
    def _():
        _cast_rows(wg_ref, wgb, 512)
        _cast_rows(wu_ref, wub, 512)

    @pl.when(blk < nused_ref[0])
    def _():
        x = x_ref[...]
        g = jnp.dot(x, wgb[...], preferred_element_type=F32)
        u = jnp.dot(x, wub[...], preferred_element_type=F32)
        o_ref[...] = (g * _sigmoid(g) * u).astype(o_ref.dtype)

    @pl.when(blk >= nused_ref[0])
    def _():
        o_ref[...] = jnp.zeros(o_ref.shape, o_ref.dtype)


def _gmm1(xs, w_gate_all, w_up_all, layer, blk_e, first, n_used):
    rows, d = xs.shape
    de = w_gate_all.shape[3]
    tn = _largest_tile(de, (512, 256, 128))
    r = MOE_ROWS
    wspec = pl.BlockSpec((None, None, d, tn), lambda j, b, be, fi, nu: (layer, be[b], 0, j))
    return pl.pallas_call(
        _gmm1_kernel,
        grid_spec=pltpu.PrefetchScalarGridSpec(
            num_scalar_prefetch=3,
            grid=(de // tn, rows // r),
            in_specs=[pl.BlockSpec((r, d), lambda j, b, be, fi, nu: (b, 0)), wspec, wspec],
            out_specs=pl.BlockSpec((r, tn), lambda j, b, be, fi, nu: (b, j)),
            scratch_shapes=[pltpu.VMEM((d, tn), BF16), pltpu.VMEM((d, tn), BF16)],
        ),
        out_shape=jax.ShapeDtypeStruct((rows, de), BF16),
        compiler_params=_cparams(("arbitrary", "arbitrary")),
        name="moe_gate_up",
    )(blk_e, first, n_used, xs, w_gate_all, w_up_all)


def _gmm2_kernel(be_ref, first_ref, nused_ref, h_ref, wd_ref, o_ref, wdb):
    blk = pl.program_id(1)

    @pl.when(first_ref[blk] == 1)
    def _():
        _cast_rows(wd_ref, wdb, 512)

    @pl.when(blk < nused_ref[0])
    def _():
        o_ref[...] = _pack_bf16_pairs(jnp.dot(h_ref[...], wdb[...], preferred_element_type=F32))

    @pl.when(blk >= nused_ref[0])
    def _():
        o_ref[...] = jnp.zeros(o_ref.shape, o_ref.dtype)


def _gmm2(h, w_down_all, layer, blk_e, first, n_used):
    rows, de = h.shape
    d = w_down_all.shape[3]
    tn = _largest_tile(d, (2048, 1024, 512, 256))
    r = MOE_ROWS
    ys = pl.pallas_call(
        _gmm2_kernel,
        grid_spec=pltpu.PrefetchScalarGridSpec(
            num_scalar_prefetch=3,
            grid=(d // tn, rows // r),
            in_specs=[
                pl.BlockSpec((r, de), lambda j, b, be, fi, nu: (b, 0)),
                pl.BlockSpec((None, None, de, tn), lambda j, b, be, fi, nu: (layer, be[b], 0, j)),
            ],
            out_specs=pl.BlockSpec((r, tn // 2), lambda j, b, be, fi, nu: (b, j)),
            scratch_shapes=[pltpu.VMEM((de, tn), BF16)],
        ),
        out_shape=jax.ShapeDtypeStruct((rows, d // 2), jnp.uint32),
        compiler_params=_cparams(("arbitrary", "arbitrary")),
        name="moe_down",
    )(blk_e, first, n_used, h, w_down_all)
    return ys, tn


def _combine_kernel(slot_ref, x_ref, wts_ref, mod_ref, ys_hbm, o_ref, buf, sem, *, pack_cols):
    tt = x_ref.shape[0]
    base = pl.program_id(0) * tt

    def issue(r, carry):
        for k in range(TOP_K):
            s = slot_ref[TOP_K * (base + r) + k]
            pltpu.make_async_copy(ys_hbm.at[pl.ds(s, 1)], buf.at[k, pl.ds(r, 1)],
                                  sem).start(priority=k % DMA_PRIORITIES)
        return carry

    lax.fori_loop(0, tt, issue, 0, unroll=DMA_ISSUE_UNROLL)
    for k in range(TOP_K):
        pltpu.make_async_copy(ys_hbm.at[pl.ds(0, tt)], buf.at[k], sem).wait()
    w = wts_ref[...]
    w0, w1 = w[:, 0:1], w[:, 1:2]
    half = pack_cols // 2
    for j in range(x_ref.shape[1] // pack_cols):
        words = slice(j * half, (j + 1) * half)
        lo0, hi0 = _unpack_bf16_pairs(buf[0, :, words])
        lo1, hi1 = _unpack_bf16_pairs(buf[1, :, words])
        for y, c0 in ((w0 * lo0 + w1 * lo1, j * pack_cols), (w0 * hi0 + w1 * hi1, j * pack_cols + half)):
            cols = slice(c0, c0 + half)
            o_ref[:, cols] = x_ref[:, cols] + mod_ref[5:6, cols] * y


def _combine(x, wts, mod_l, row_map, ys, slots, pack_cols):
    n, d = x.shape
    tt = TOKEN_TILE
    return pl.pallas_call(
        functools.partial(_combine_kernel, pack_cols=pack_cols),
        grid_spec=pltpu.PrefetchScalarGridSpec(
            num_scalar_prefetch=1,
            grid=(n // tt,),
            in_specs=[
                pl.BlockSpec((tt, d), lambda i, s: (i, 0)),
                pl.BlockSpec((tt, LANES), lambda i, s: (i, 0)),
                pl.BlockSpec((None, 6, d), lambda i, s: (row_map(i), 0, 0)),
                pl.BlockSpec(memory_space=pl.ANY),
            ],
            out_specs=pl.BlockSpec((tt, d), lambda i, s: (i, 0)),
            scratch_shapes=[pltpu.VMEM((TOP_K, tt, d // 2), jnp.uint32), pltpu.SemaphoreType.DMA(())],
        ),
        out_shape=jax.ShapeDtypeStruct((n, d), F32),
        compiler_params=_cparams(("arbitrary",)),
        name="moe_combine",
    )(slots, x, wts, mod_l, ys)


def _rope_tables(seq):
    rows = seq // GRID_W
    row = jnp.repeat(jnp.arange(rows), GRID_W).astype(F32)
    col = jnp.tile(jnp.arange(GRID_W), rows).astype(F32)
    n_freq = HEAD_DIM // 4
    inv = 1.0 / (ROPE_BASE ** (jnp.arange(n_freq, dtype=F32) / n_freq))
    ang = jnp.concatenate([row[:, None] * inv, col[:, None] * inv], axis=-1)
    cos, sin = jnp.cos(ang), jnp.sin(ang)
    return jnp.concatenate([cos, cos], axis=-1), jnp.concatenate([-sin, sin], axis=-1)


def kernel(x_prompt, x_sample, c, cache_win_k, cache_win_v, cache_diff_k, cache_diff_v, state_rnn, c_ctx, w_mod, b_mod, norm1_g, norm2_g, final_g, w_in, conv_w, conv_b, rg_wa, rg_ba, rg_wx, rg_bx, rg_lambda, win_sink, diff_lambda, diff_subln_g, w_branch, w_out, moe_w_group, moe_b_group, moe_w_router, moe_b_router, moe_w_gate, moe_w_up, moe_w_down):
    bp, tp, d = x_prompt.shape
    bs, ts, _ = x_sample.shape
    depth = w_in.shape[0]
    past = cache_win_k.shape[2]
    n_kv = cache_win_k.shape[3]
    n_win = win_sink.shape[1]
    group = n_win // n_kv
    n_diff = cache_diff_k.shape[3]
    d_rnn = rg_ba.shape[2]
    n_groups = moe_w_group.shape[2]
    n_experts = moe_w_router.shape[2]
    per_group = n_experts // n_groups
    n_p, n_s = bp * tp, bs * ts
    tt = TOKEN_TILE
    assert tp % tt == 0 and ts % tt == 0 and n_p % ts == 0 and d_rnn == d
    assert n_groups + n_experts <= LANES

    sections = (d_rnn, d_rnn, n_win * HEAD_DIM, n_kv * HEAD_DIM, n_kv * HEAD_DIM,
                n_diff * 2 * HEAD_DIM, n_diff * 2 * HEAD_DIM, n_diff * 2 * HEAD_DIM, 3 * d)
    offs = [0]
    for s_ in sections:
        offs.append(offs[-1] + s_)
    _, _, o_wq, o_wk, o_wv, o_dq, o_dk, o_dv, o_mg = offs[:9]

    row_map = _mod_row_map(n_p // tt, ts // tt)

    n_rows = -(-(1 + bs) // SUBLANES) * SUBLANES
    cvecs = jnp.zeros((n_rows, d), F32).at[0].set(c_ctx).at[1:1 + bs].set(c)
    mod = _ada_mod_all(cvecs, w_mod, b_mod).reshape(depth, n_rows, 6, d)

    cos_f, sin_f = _rope_tables(ts)
    ck_win = cache_win_k.reshape(bs, depth, past, n_kv * HEAD_DIM)
    cv_win = cache_win_v.reshape(bs, depth, past, n_kv * HEAD_DIM)
    ck_diff = cache_diff_k.reshape(bs, depth, past, n_diff * 2 * HEAD_DIM)
    cv_diff = cache_diff_v.reshape(bs, depth, past, n_diff * 2 * HEAD_DIM)
    w_rt = jnp.zeros((depth, d, LANES), F32).at[:, :, :n_groups].set(moe_w_group) \
        .at[:, :, n_groups:n_groups + n_experts].set(moe_w_router)
    b_rt = jnp.zeros((depth, 1, LANES), F32).at[:, 0, :n_groups].set(moe_b_group) \
        .at[:, 0, n_groups:n_groups + n_experts].set(moe_b_router)
    w_rt_hi = w_rt.astype(BF16)
    w_rt_lo = (w_rt - w_rt_hi.astype(F32)).astype(BF16)
    h0_prompt = jnp.zeros((bp, 2, d_rnn), F32)
    norm1_g = norm1_g.reshape(depth, 1, d)
    norm2_g = norm2_g.reshape(depth, 1, d)
    conv_b = conv_b.reshape(depth, 1, d_rnn)
    diff_subln_g = diff_subln_g.reshape(depth, 1, 2 * HEAD_DIM)

    x = jnp.concatenate([x_prompt.reshape(n_p, d), x_sample.reshape(n_s, d)], axis=0)
    new_wk, new_wv, new_dk, new_dv, new_st = [], [], [], [], []
    for l in range(depth):
        lam_init = 0.8 - 0.6 * math.exp(-0.3 * l)
        mod_l = mod[l]
        hn = _norm1(x, norm1_g, l, mod_l, row_map)
        proj = _in_proj(hn, w_in, l)

        rg_args = (conv_w, conv_b, rg_wa, rg_wx, rg_ba, rg_bx, rg_lambda)
        oa_p, st_p = _rglru(proj, 0, bp, tp, d_rnn, *rg_args, h0_prompt, l)
        oa_s, _ = _rglru(proj, n_p // ts, bs, ts, d_rnn, *rg_args, state_rnn[:, l], l)

        ob_p = _win_ctx(proj, bp, tp, o_wq, o_wk, o_wv, n_kv, group, win_sink, l)
        ob_s = _win_lat(proj, n_p, bs, ts, o_wq, o_wk, o_wv, n_kv, group, win_sink, l,
                        ck_win, cv_win, cos_f, sin_f)
        oc_p = _diff_ctx(proj, bp, tp, o_dq, o_dk, o_dv, n_diff, diff_lambda, diff_subln_g, l, lam_init)
        oc_s = _diff_lat(proj, n_p, bs, ts, o_dq, o_dk, o_dv, n_diff, diff_lambda, diff_subln_g, l,
                         lam_init, ck_diff, cv_diff, cos_f, sin_f)

        merged = _merge((oa_p, ob_p, oc_p), (oa_s, ob_s, oc_s), proj, o_mg, w_branch, l)
        x = _out_proj(merged, w_out, l, x, mod_l, row_map)

        hn2, eid, wts, cnt = _norm2_router(x, norm2_g, l, mod_l, row_map, w_rt_hi[l], w_rt_lo[l],
                                           b_rt[l], n_groups, per_group)
        counts = cnt[0, n_groups:n_groups + n_experts].astype(jnp.int32)
        slot_tok, dest, blk_e, first, n_used = _moe_plan(
            eid[:, :TOP_K], eid[:, TOP_K:2 * TOP_K], counts, n_experts, MOE_ROWS)
        xs = _gather_rows(hn2, slot_tok)
        hmid = _gmm1(xs, moe_w_gate, moe_w_up, l, blk_e, first, n_used)
        ys, pack_cols = _gmm2(hmid, moe_w_down, l, blk_e, first, n_used)
        x = _combine(x, wts, mod_l, row_map, ys, dest, pack_cols)

        pr = proj[:n_p]
        new_wk.append(pr[:, o_wk:o_wv].reshape(bp, tp, n_kv, HEAD_DIM))
        new_wv.append(pr[:, o_wv:o_dq].reshape(bp, tp, n_kv, HEAD_DIM))
        new_dk.append(pr[:, o_dk:o_dv].reshape(bp, tp, n_diff, 2, HEAD_DIM))
        new_dv.append(pr[:, o_dv:o_mg].reshape(bp, tp, n_diff, 2 * HEAD_DIM))
        new_st.append(st_p)

    y = _final_norm(x, final_g)
    return (y[:n_p].reshape(bp, tp, d), y[n_p:].reshape(bs, ts, d),
            jnp.stack(new_wk, axis=1), jnp.stack(new_wv, axis=1),
            jnp.stack(new_dk, axis=1), jnp.stack(new_dv, axis=1), jnp.stack(new_st, axis=1))
```
